```python
import jax, jax.numpy as jnp
from jax import lax
import numpy as np

D_MODEL = 2048
BATCH = 2
SEQ = 8192
DEPTH = 2

D_MIX = D_MODEL
D_SGU = D_MIX // 2
D_NA = D_MIX - D_SGU
SGU_CHUNK = 128
SGU_GROUP = 128
SGU_GROUPS = D_SGU // SGU_GROUP
NA_HEAD_DIM = 64
NA_HEADS = D_NA // NA_HEAD_DIM
GRID_W = 64
NA_KH_MAX = 8
NA_KW = 16
NA_QC = 16
NA_KC = NA_QC + NA_KW
IN_SPLITS = (D_SGU, 2 * D_SGU, 3 * D_SGU, 3 * D_SGU + D_NA, 3 * D_SGU + 2 * D_NA, 3 * D_SGU + 3 * D_NA)
D_IN = 3 * D_SGU + 4 * D_NA
DEEPNORM_ALPHA = (2 * DEPTH) ** 0.25
DEEPNORM_BETA = (8 * DEPTH) ** -0.25
ADA_SCALE = 0.1
LN_EPS = 1e-5

kernel_name = "hybrid_sgu_natten_deepnorm_adaln"


def _layernorm(x, g=None, b=None):
    xf = x.astype(jnp.float32)
    mu = jnp.mean(xf, axis=-1, keepdims=True)
    var = jnp.mean(jnp.square(xf - mu), axis=-1, keepdims=True)
    y = (xf - mu) * lax.rsqrt(var + LN_EPS)
    if g is not None:
        y = y * g.astype(jnp.float32) + b.astype(jnp.float32)
    return y.astype(x.dtype)


def _spatial_gating(u, v, z, norm_g, norm_b, w_s, b_s):
    B, T, _ = u.shape
    n_chunks = T // SGU_CHUNK
    vg = v.reshape(B, T, SGU_GROUPS, SGU_GROUP)
    vg = _layernorm(vg, norm_g.reshape(SGU_GROUPS, SGU_GROUP), norm_b.reshape(SGU_GROUPS, SGU_GROUP))
    vc = vg.reshape(B, n_chunks, SGU_CHUNK, SGU_GROUPS, SGU_GROUP)
    sv = jnp.einsum('gpq,bnqgc->bnpgc', w_s, vc) + b_s.T[None, None, :, :, None]
    return u * sv.reshape(B, T, D_SGU) * jax.nn.silu(z)


def _na_tables(rows):
    kh = min(NA_KH_MAX, rows)
    r = np.arange(rows)
    row_start = np.clip(r - kh // 2, 0, rows - kh)
    row_bias_idx = row_start[:, None] + np.arange(kh)[None, :] - r[:, None] + NA_KH_MAX - 1
    ncb = GRID_W // NA_QC
    qc0 = np.arange(ncb) * NA_QC
    band_start = np.clip(qc0 - NA_KW // 2, 0, GRID_W - NA_KC)
    band_cols = band_start[:, None] + np.arange(NA_KC)[None, :]
    q_cols = qc0[:, None] + np.arange(NA_QC)[None, :]
    win_start = np.clip(q_cols - NA_KW // 2, 0, GRID_W - NA_KW)
    col_valid = ((band_cols[:, None, :] >= win_start[:, :, None])
                 & (band_cols[:, None, :] < win_start[:, :, None] + NA_KW))
    col_bias_idx = np.clip(band_cols[:, None, :] - q_cols[:, :, None] + NA_KW - 1, 0, 2 * NA_KW - 2)
    return (kh, row_start.astype(np.int32), row_bias_idx.astype(np.int32),
            band_cols.astype(np.int32), col_valid, col_bias_idx.astype(np.int32))


def _neighbourhood_attention(q, k, v, rpb):
    B, T, H, dh = q.shape
    rows = T // GRID_W
    kh, row_start, row_bias_idx, band_cols, col_valid, col_bias_idx = _na_tables(rows)
    ncb = GRID_W // NA_QC
    to_grid = lambda a: a.reshape(B, rows, GRID_W, H, dh).transpose(0, 3, 1, 2, 4)
    kg, vg = to_grid(k), to_grid(v)
    q_rows = to_grid(q).reshape(B, H, rows, ncb, NA_QC, dh).transpose(2, 0, 1, 3, 4, 5)
    valid = jnp.asarray(col_valid)[None, None, :, :, None, :]
    col_bias_idx = jnp.asarray(col_bias_idx)[:, :, None, :]
    band_cols = jnp.asarray(band_cols)
    scale = dh ** -0.5

    def one_row(args):
        q_r, rs, rbi = args
        k_r = lax.dynamic_slice_in_dim(kg, rs, kh, axis=2)
        v_r = lax.dynamic_slice_in_dim(vg, rs, kh, axis=2)
        k_b = jnp.take(k_r, band_cols, axis=3)
        v_b = jnp.take(v_r, band_cols, axis=3)
        s = jnp.einsum('bhcqd,bhicxd->bhcqix', q_r, k_b).astype(jnp.float32) * scale
        bias = rpb[:, rbi[None, None, :, None], col_bias_idx]
        s = jnp.where(valid, s + bias[None].astype(jnp.float32), -1e30)
        p = jax.nn.softmax(s.reshape(B, H, ncb, NA_QC, kh * NA_KC), axis=-1)
        p = p.reshape(s.shape).astype(v.dtype)
        return jnp.einsum('bhcqix,bhicxd->bhcqd', p, v_b)

    out = lax.map(one_row, (q_rows, jnp.asarray(row_start), jnp.asarray(row_bias_idx)))
    out = out.reshape(rows, B, H, GRID_W, dh).transpose(1, 0, 3, 2, 4)
    return out.reshape(B, T, H * dh)


def setup_inputs(seed: int = 0) -> dict:
    key = jax.random.key(seed)
    ks = jax.random.split(key, 14)
    f32 = jnp.float32
    n = lambda k, s: jax.random.normal(k, s, f32)
    return {
        "x": n(ks[0], (BATCH, SEQ, D_MODEL)),
        "c": n(ks[1], (BATCH, D_MODEL)),
        "w_ada": n(ks[2], (DEPTH, D_MODEL, 3 * D_MODEL)) * (D_MODEL ** -0.5) * ADA_SCALE,
        "b_ada": n(ks[3], (DEPTH, 3 * D_MODEL)) * 0.01,
        "w_in": n(ks[4], (DEPTH, D_MODEL, D_IN)) * (D_MODEL ** -0.5),
        "sgu_norm_g": 1.0 + 0.02 * n(ks[5], (DEPTH, D_SGU)),
        "sgu_norm_b": 0.02 * n(ks[6], (DEPTH, D_SGU)),
        "w_spatial": n(ks[7], (DEPTH, SGU_GROUPS, SGU_CHUNK, SGU_CHUNK)) * (SGU_CHUNK ** -0.5),
        "b_spatial": 1.0 + 0.02 * n(ks[8], (DEPTH, SGU_GROUPS, SGU_CHUNK)),
        "rpb": 0.1 * n(ks[9], (DEPTH, NA_HEADS, 2 * NA_KH_MAX - 1, 2 * NA_KW - 1)),
        "w_out": n(ks[10], (DEPTH, D_MIX, D_MODEL)) * (D_MIX ** -0.5) * DEEPNORM_BETA,
        "ln_g": 1.0 + 0.02 * n(ks[11], (DEPTH, D_MODEL)),
        "ln_b": 0.02 * n(ks[12], (DEPTH, D_MODEL)),
    }


def reference(x, c, w_ada, b_ada, w_in, sgu_norm_g, sgu_norm_b, w_spatial, b_spatial, rpb, w_out, ln_g, ln_b):
    B, T, _ = x.shape
    for l in range(DEPTH):
        mod = jax.nn.silu(c) @ w_ada[l] + b_ada[l]
        shift, scale, gate = jnp.split(mod[:, None, :], 3, axis=-1)
        h = _layernorm(x) * (1.0 + scale) + shift
        proj = h @ w_in[l]
        u, v, z_a, q, k, v_b, z_b = jnp.split(proj, IN_SPLITS, axis=-1)
        y_a = _spatial_gating(jax.nn.gelu(u, approximate=False), jax.nn.gelu(v, approximate=False), z_a,
                              sgu_norm_g[l], sgu_norm_b[l], w_spatial[l], b_spatial[l])
        hs = (B, T, NA_HEADS, NA_HEAD_DIM)
        y_b = _neighbourhood_attention(q.reshape(hs), k.reshape(hs), v_b.reshape(hs), rpb[l]) * jax.nn.silu(z_b)
        y = jnp.concatenate([y_a, y_b], axis=-1) @ w_out[l]
        x = _layernorm(DEEPNORM_ALPHA * x + (1.0 + gate) * y, ln_g[l], ln_b[l])
    return x
```

```python
import functools

import numpy as np
import jax
import jax.numpy as jnp
from jax import lax
from jax.experimental import pallas as pl
from jax.experimental.pallas import tpu as pltpu

D_MODEL = 2048
DEPTH = 2
D_SGU = 1024
D_NA = 1024
SGU_CHUNK = 128
SGU_GROUP = 128
SGU_GROUPS = D_SGU // SGU_GROUP
NA_HEAD_DIM = 64
NA_HEADS = D_NA // NA_HEAD_DIM
GRID_W = 64
NA_KH = 8
NA_KW = 16
D_IN = 3 * D_SGU + 4 * D_NA
DEEPNORM_ALPHA = (2 * DEPTH) ** 0.25
LN_EPS = 1e-5
NEG_INF = -1e30

LANES = 128
MXU_N = 256

ADA_TN = 1024
PROJ_TM = 1024
PROJ_TN = MXU_N
SGU_STEPS = D_SGU // PROJ_TN
NA_STEPS = 4 * D_NA // PROJ_TN
LN_ROWS = 32
OUT_TM = 512
HEADS_PER_STEP = LANES // NA_HEAD_DIM
RHO_TILES = 2 * NA_KH - 2
VMEM_LIMIT = 56 * 1024 * 1024


def _silu(v):
    return v * jax.nn.sigmoid(v)


def _gelu(v):
    return 0.5 * v * (1.0 + lax.erf(v * (2.0 ** -0.5)))


def _ada_kernel(c_ref, w_ref, b_ref, o_ref):
    s = _silu(c_ref[...]).astype(jnp.bfloat16)
    w = w_ref[0].astype(jnp.bfloat16)
    o_ref[0] = jnp.dot(s, w, preferred_element_type=jnp.float32) + b_ref[0]


def _ada_modulation(c_pad, w_ada, b_ada):
    rows = c_pad.shape[0]
    n_out = w_ada.shape[-1]
    return pl.pallas_call(
        _ada_kernel,
        grid=(DEPTH, n_out // ADA_TN),
        in_specs=[
            pl.BlockSpec((rows, D_MODEL), lambda l, j: (0, 0)),
            pl.BlockSpec((1, D_MODEL, ADA_TN), lambda l, j: (l, 0, j)),
            pl.BlockSpec((1, 1, ADA_TN), lambda l, j: (l, 0, j)),
        ],
        out_specs=pl.BlockSpec((1, rows, ADA_TN), lambda l, j: (l, 0, j)),
        out_shape=jax.ShapeDtypeStruct((DEPTH, rows, n_out), jnp.float32),
        compiler_params=pltpu.CompilerParams(
            dimension_semantics=("arbitrary", "arbitrary"), vmem_limit_bytes=VMEM_LIMIT),
        name="ada_modulation",
    )(c_pad, w_ada, b_ada.reshape(DEPTH, 1, n_out))


def _proj_kernel(x_ref, mod_ref, wu_ref, wv_ref, wz_ref, wn_ref, ng_ref, nb_ref, ws_ref, bs_ref,
                 ya_ref, na_ref, h_ref, u_ref, v_ref, z_ref):
    j = pl.program_id(1)

    @pl.when(j == 0)
    def _prologue():
        shift = mod_ref[0, :, 0:D_MODEL]
        scale1 = 1.0 + mod_ref[0, :, D_MODEL:2 * D_MODEL]

        def ln_chunk(ci, carry):
            rows = pl.ds(pl.multiple_of(ci * LN_ROWS, LN_ROWS), LN_ROWS)
            xv = x_ref[rows, :]
            mu = jnp.mean(xv, axis=-1, keepdims=True)
            d = xv - mu
            var = jnp.mean(d * d, axis=-1, keepdims=True)
            hn = d * lax.rsqrt(var + LN_EPS)
            h_ref[rows, :] = (hn * scale1 + shift).astype(jnp.bfloat16)
            return carry

        lax.fori_loop(0, PROJ_TM // LN_ROWS, ln_chunk, 0)

    @pl.when(j < SGU_STEPS)
    def _sgu_step():
        h = h_ref[...]
        u_ref[...] = jnp.dot(h, wu_ref[...], preferred_element_type=jnp.float32)
        v_ref[...] = jnp.dot(h, wv_ref[...], preferred_element_type=jnp.float32)
        z_ref[...] = jnp.dot(h, wz_ref[...], preferred_element_type=jnp.float32)

        def chunk(ci, carry):
            rows = pl.ds(pl.multiple_of(ci * SGU_CHUNK, SGU_CHUNK), SGU_CHUNK)
            for g in range(PROJ_TN // SGU_GROUP):
                cols = slice(g * SGU_GROUP, (g + 1) * SGU_GROUP)
                gv = _gelu(v_ref[rows, cols])
                mu = jnp.mean(gv, axis=-1, keepdims=True)
                d = gv - mu
                var = jnp.mean(d * d, axis=-1, keepdims=True)
                vn = d * lax.rsqrt(var + LN_EPS) * ng_ref[:, cols] + nb_ref[:, cols]
                sv = jnp.dot(ws_ref[g], vn.astype(jnp.bfloat16),
                             preferred_element_type=jnp.float32) + bs_ref[g]
                ya = _gelu(u_ref[rows, cols]) * sv * _silu(z_ref[rows, cols])
                ya_ref[rows, cols] = ya.astype(jnp.bfloat16)
            return carry

        lax.fori_loop(0, PROJ_TM // SGU_CHUNK, chunk, 0)

    @pl.when(j >= SGU_STEPS)
    def _na_step():
        r = jnp.dot(h_ref[...], wn_ref[...], preferred_element_type=jnp.float32)
        role = (j - SGU_STEPS) // (D_NA // PROJ_TN)

        @pl.when(role == 3)
        def _gate():
            na_ref[...] = _silu(r).astype(jnp.bfloat16)

        @pl.when(role != 3)
        def _qkv():
            qscale = jnp.where(role == 0, NA_HEAD_DIM ** -0.5, 1.0).astype(jnp.float32)
            na_ref[...] = (r * qscale).astype(jnp.bfloat16)


def _in_projection(x2, mod_l, w_in_bf, norm_g, norm_b, ws_bf, bs_full, seq_len):
    m = x2.shape[0]
    blocks_per_batch = seq_len // PROJ_TM
    sgu_j = lambda j: jnp.minimum(j, SGU_STEPS - 1)
    na_j = lambda j: jnp.maximum(j - SGU_STEPS, 0)
    groups_per_step = PROJ_TN // SGU_GROUP
    n_v = D_SGU // PROJ_TN
    return pl.pallas_call(
        _proj_kernel,
        grid=(m // PROJ_TM, SGU_STEPS + NA_STEPS),
        in_specs=[
            pl.BlockSpec((PROJ_TM, D_MODEL), lambda i, j: (i, 0)),
            pl.BlockSpec((1, 1, 3 * D_MODEL), lambda i, j: (i // blocks_per_batch, 0, 0)),
            pl.BlockSpec((D_MODEL, PROJ_TN), lambda i, j: (0, sgu_j(j))),
            pl.BlockSpec((D_MODEL, PROJ_TN), lambda i, j: (0, n_v + sgu_j(j))),
            pl.BlockSpec((D_MODEL, PROJ_TN), lambda i, j: (0, 2 * n_v + sgu_j(j))),
            pl.BlockSpec((D_MODEL, PROJ_TN), lambda i, j: (0, 3 * n_v + na_j(j))),
            pl.BlockSpec((1, PROJ_TN), lambda i, j: (0, sgu_j(j))),
            pl.BlockSpec((1, PROJ_TN), lambda i, j: (0, sgu_j(j))),
            pl.BlockSpec((groups_per_step, SGU_CHUNK, SGU_CHUNK), lambda i, j: (sgu_j(j), 0, 0)),
            pl.BlockSpec((groups_per_step, SGU_CHUNK, SGU_GROUP), lambda i, j: (sgu_j(j), 0, 0)),
        ],
        out_specs=[
            pl.BlockSpec((PROJ_TM, PROJ_TN), lambda i, j: (i, sgu_j(j))),
            pl.BlockSpec((PROJ_TM, PROJ_TN), lambda i, j: (i, na_j(j))),
        ],
        out_shape=[
            jax.ShapeDtypeStruct((m, D_SGU), jnp.bfloat16),
            jax.ShapeDtypeStruct((m, 4 * D_NA), jnp.bfloat16),
        ],
        scratch_shapes=[
            pltpu.VMEM((PROJ_TM, D_MODEL), jnp.bfloat16),
            pltpu.VMEM((PROJ_TM, PROJ_TN), jnp.float32),
            pltpu.VMEM((PROJ_TM, PROJ_TN), jnp.float32),
            pltpu.VMEM((PROJ_TM, PROJ_TN), jnp.float32),
        ],
        compiler_params=pltpu.CompilerParams(
            dimension_semantics=("arbitrary", "arbitrary"), vmem_limit_bytes=VMEM_LIMIT),
        name="in_projection",
    )(x2, mod_l, w_in_bf, w_in_bf, w_in_bf, w_in_bf, norm_g, norm_b, ws_bf, bs_full)


def _attn_kernel(q_ref, k_ref, v_ref, z_ref, t_ref, y_ref, *, rows):
    lane = lax.broadcasted_iota(jnp.int32, (GRID_W, LANES), 1)
    first_head = lane < NA_HEAD_DIM
    win = NA_KH * GRID_W

    def row_body(r, carry):
        rs = jnp.clip(r - NA_KH // 2, 0, rows - NA_KH)
        rho0 = rs - r + NA_KH - 1
        q_rows = pl.ds(pl.multiple_of(r * GRID_W, GRID_W), GRID_W)
        k_rows = pl.ds(pl.multiple_of(rs * GRID_W, GRID_W), win)
        q = q_ref[q_rows, :]
        zero = jnp.zeros_like(q)
        qq = jnp.concatenate([jnp.where(first_head, q, zero), jnp.where(first_head, zero, q)], axis=0)
        s = lax.dot_general(qq, k_ref[k_rows, :], (((1,), (1,)), ((), ())),
                            preferred_element_type=jnp.float32)
        bias = jnp.concatenate(
            [jnp.concatenate([t_ref[hh, rho0 + 2 * jt] for jt in range(NA_KH // 2)], axis=1)
             for hh in range(HEADS_PER_STEP)], axis=0)
        s = s + bias
        mx = jnp.max(s, axis=-1, keepdims=True)
        p = jnp.exp(s - mx)
        den = jnp.sum(p, axis=-1, keepdims=True)
        o = jnp.dot(p.astype(jnp.bfloat16), v_ref[k_rows, :], preferred_element_type=jnp.float32)
        o = o / den
        out = jnp.where(first_head, o[:GRID_W], o[GRID_W:])
        y_ref[q_rows, :] = (out * z_ref[q_rows, :].astype(jnp.float32)).astype(jnp.bfloat16)
        return carry

    lax.fori_loop(0, rows, row_body, 0, unroll=2)


def _attention(na, table, batch, seq_len):
    rows = seq_len // GRID_W
    n_hp = NA_HEADS // HEADS_PER_STEP
    col = lambda part: (lambda b, hp: (b, part * n_hp + hp))
    return pl.pallas_call(
        functools.partial(_attn_kernel, rows=rows),
        grid=(batch, n_hp),
        in_specs=[
            pl.BlockSpec((seq_len, LANES), col(0)),
            pl.BlockSpec((seq_len, LANES), col(1)),
            pl.BlockSpec((seq_len, LANES), col(2)),
            pl.BlockSpec((seq_len, LANES), col(3)),
            pl.BlockSpec((HEADS_PER_STEP, RHO_TILES, GRID_W, LANES), lambda b, hp: (hp, 0, 0, 0)),
        ],
        out_specs=pl.BlockSpec((seq_len, LANES), lambda b, hp: (b, hp)),
        out_shape=jax.ShapeDtypeStruct((batch * seq_len, D_NA), jnp.bfloat16),
        compiler_params=pltpu.CompilerParams(
            dimension_semantics=("arbitrary", "arbitrary"), vmem_limit_bytes=VMEM_LIMIT),
        name="neighbourhood_attention",
    )(na, na, na, na, table)


def _bias_tables(rpb):
    qc = np.arange(GRID_W)[:, None]
    xc = np.arange(GRID_W)[None, :]
    win_start = np.clip(qc - NA_KW // 2, 0, GRID_W - NA_KW)
    valid = (xc >= win_start) & (xc < win_start + NA_KW)
    cidx = np.clip(xc - qc + NA_KW - 1, 0, 2 * NA_KW - 2)
    bt = jnp.where(jnp.asarray(valid), rpb[:, :, :, cidx], NEG_INF)
    return jnp.concatenate([bt[:, :, :-1], bt[:, :, 1:]], axis=-1)


def _out_kernel(ya_ref, yb_ref, w_ref, x_ref, mod_ref, g_ref, b_ref, o_ref):
    y = jnp.dot(ya_ref[...], w_ref[0:D_SGU, :], preferred_element_type=jnp.float32)
    y = y + jnp.dot(yb_ref[...], w_ref[D_SGU:, :], preferred_element_type=jnp.float32)
    gate1 = 1.0 + mod_ref[0, :, 2 * D_MODEL:3 * D_MODEL]
    t = DEEPNORM_ALPHA * x_ref[...] + gate1 * y
    mu = jnp.mean(t, axis=-1, keepdims=True)
    d = t - mu
    var = jnp.mean(d * d, axis=-1, keepdims=True)
    o_ref[...] = d * lax.rsqrt(var + LN_EPS) * g_ref[...] + b_ref[...]


def _out_projection(ya, yb, w_out_bf, x2, mod_l, ln_g, ln_b, seq_len):
    m = x2.shape[0]
    blocks_per_batch = seq_len // OUT_TM
    return pl.pallas_call(
        _out_kernel,
        grid=(m // OUT_TM,),
        in_specs=[
            pl.BlockSpec((OUT_TM, D_SGU), lambda i: (i, 0)),
            pl.BlockSpec((OUT_TM, D_NA), lambda i: (i, 0)),
            pl.BlockSpec((D_SGU + D_NA, D_MODEL), lambda i: (0, 0)),
            pl.BlockSpec((OUT_TM, D_MODEL), lambda i: (i, 0)),
            pl.BlockSpec((1, 1, 3 * D_MODEL), lambda i: (i // blocks_per_batch, 0, 0)),
            pl.BlockSpec((1, D_MODEL), lambda i: (0, 0)),
            pl.BlockSpec((1, D_MODEL), lambda i: (0, 0)),
        ],
        out_specs=pl.BlockSpec((OUT_TM, D_MODEL), lambda i: (i, 0)),
        out_shape=jax.ShapeDtypeStruct((m, D_MODEL), jnp.float32),
        compiler_params=pltpu.CompilerParams(
            dimension_semantics=("arbitrary",), vmem_limit_bytes=VMEM_LIMIT),
        name="out_projection",
    )(ya, yb, w_out_bf, x2, mod_l, ln_g, ln_b)


def kernel(x, c, w_ada, b_ada, w_in, sgu_norm_g, sgu_norm_b, w_spatial, b_spatial, rpb, w_out, ln_g, ln_b):
    batch, seq_len, _ = x.shape
    assert seq_len % PROJ_TM == 0 and seq_len % OUT_TM == 0 and seq_len % (GRID_W * NA_KH) == 0
    x2 = x.reshape(batch * seq_len, D_MODEL)
    c_pad = jnp.pad(c, ((0, 8 - batch), (0, 0)))
    mod = _ada_modulation(c_pad, w_ada, b_ada)[:, :batch].reshape(DEPTH, batch, 1, 3 * D_MODEL)
    tables = _bias_tables(rpb)
    for l in range(DEPTH):
        bs_full = jnp.broadcast_to(b_spatial[l][:, :, None], (SGU_GROUPS, SGU_CHUNK, SGU_GROUP))
        ya, na = _in_projection(
            x2, mod[l], w_in[l].astype(jnp.bfloat16), sgu_norm_g[l][None], sgu_norm_b[l][None],
            w_spatial[l].astype(jnp.bfloat16), bs_full, seq_len)
        yb = _attention(na, tables[l], batch, seq_len)
        x2 = _out_projection(ya, yb, w_out[l].astype(jnp.bfloat16), x2, mod[l],
                             ln_g[l][None], ln_b[l][None], seq_len)
    return x2.reshape(batch, seq_len, D_MODEL)
```

```python
import functools

import numpy as np
import jax
import jax.numpy as jnp
from jax import lax
from jax.experimental import pallas as pl
from jax.experimental.pallas import tpu as pltpu

D_MODEL = 2048
DEPTH = 2
D_SGU = 1024
D_NA = 1024
SGU_CHUNK = 128
SGU_GROUP = 128
SGU_GROUPS = D_SGU // SGU_GROUP
NA_HEAD_DIM = 64
NA_HEADS = D_NA // NA_HEAD_DIM
GRID_W = 64
NA_KH = 8
NA_KW = 16
D_IN = 3 * D_SGU + 4 * D_NA
DEEPNORM_ALPHA = (2 * DEPTH) ** 0.25
LN_EPS = 1e-5
NEG_INF = -1e30

LANES = 128
MXU_N = 256

ADA_TN = 1024
PROJ_TM = 1024
PROJ_TN = MXU_N
SGU_STEPS = D_SGU // PROJ_TN
NA_STEPS = 4 * D_NA // PROJ_TN
LN_ROWS = 32
OUT_TM = 512
HEADS_PER_STEP = LANES // NA_HEAD_DIM
RHO_TILES = 2 * NA_KH - 2
ATTN_G = 4
VMEM_LIMIT = 56 * 1024 * 1024


def _silu(v):
    return v * jax.nn.sigmoid(v)


def _gelu(v):
    return 0.5 * v * (1.0 + lax.erf(v * (2.0 ** -0.5)))


def _ada_kernel(c_ref, w_ref, b_ref, o_ref):
    s = _silu(c_ref[...]).astype(jnp.bfloat16)
    w = w_ref[0].astype(jnp.bfloat16)
    o_ref[0] = jnp.dot(s, w, preferred_element_type=jnp.float32) + b_ref[0]


def _ada_modulation(c_pad, w_ada, b_ada):
    rows = c_pad.shape[0]
    n_out = w_ada.shape[-1]
    return pl.pallas_call(
        _ada_kernel,
        grid=(DEPTH, n_out // ADA_TN),
        in_specs=[
            pl.BlockSpec((rows, D_MODEL), lambda l, j: (0, 0)),
            pl.BlockSpec((1, D_MODEL, ADA_TN), lambda l, j: (l, 0, j)),
            pl.BlockSpec((1, 1, ADA_TN), lambda l, j: (l, 0, j)),
        ],
        out_specs=pl.BlockSpec((1, rows, ADA_TN), lambda l, j: (l, 0, j)),
        out_shape=jax.ShapeDtypeStruct((DEPTH, rows, n_out), jnp.float32),
        compiler_params=pltpu.CompilerParams(
            dimension_semantics=("arbitrary", "arbitrary"), vmem_limit_bytes=VMEM_LIMIT),
        name="ada_modulation",
    )(c_pad, w_ada, b_ada.reshape(DEPTH, 1, n_out))


def _proj_kernel(x_ref, mod_ref, wu_ref, wv_ref, wz_ref, wn_ref, ng_ref, nb_ref, ws_ref, bs_ref,
                 ya_ref, na_ref, h_ref, u_ref, v_ref, z_ref):
    j = pl.program_id(1)

    @pl.when(j == 0)
    def _prologue():
        shift = mod_ref[0, :, 0:D_MODEL]
        scale1 = 1.0 + mod_ref[0, :, D_MODEL:2 * D_MODEL]

        def ln_chunk(ci, carry):
            rows = pl.ds(pl.multiple_of(ci * LN_ROWS, LN_ROWS), LN_ROWS)
            xv = x_ref[rows, :]
            mu = jnp.mean(xv, axis=-1, keepdims=True)
            d = xv - mu
            var = jnp.mean(d * d, axis=-1, keepdims=True)
            hn = d * lax.rsqrt(var + LN_EPS)
            h_ref[rows, :] = (hn * scale1 + shift).astype(jnp.bfloat16)
            return carry

        lax.fori_loop(0, PROJ_TM // LN_ROWS, ln_chunk, 0)

    @pl.when(j < SGU_STEPS)
    def _sgu_step():
        h = h_ref[...]
        u_ref[...] = jnp.dot(h, wu_ref[...], preferred_element_type=jnp.float32)
        v_ref[...] = jnp.dot(h, wv_ref[...], preferred_element_type=jnp.float32)
        z_ref[...] = jnp.dot(h, wz_ref[...], preferred_element_type=jnp.float32)

        def chunk(ci, carry):
            rows = pl.ds(pl.multiple_of(ci * SGU_CHUNK, SGU_CHUNK), SGU_CHUNK)
            for g in range(PROJ_TN // SGU_GROUP):
                cols = slice(g * SGU_GROUP, (g + 1) * SGU_GROUP)
                gv = _gelu(v_ref[rows, cols])
                mu = jnp.mean(gv, axis=-1, keepdims=True)
                d = gv - mu
                var = jnp.mean(d * d, axis=-1, keepdims=True)
                vn = d * lax.rsqrt(var + LN_EPS) * ng_ref[:, cols] + nb_ref[:, cols]
                sv = jnp.dot(ws_ref[g], vn.astype(jnp.bfloat16),
                             preferred_element_type=jnp.float32) + bs_ref[g]
                ya = _gelu(u_ref[rows, cols]) * sv * _silu(z_ref[rows, cols])
                ya_ref[rows, cols] = ya.astype(jnp.bfloat16)
            return carry

        lax.fori_loop(0, PROJ_TM // SGU_CHUNK, chunk, 0)

    @pl.when(j >= SGU_STEPS)
    def _na_step():
        r = jnp.dot(h_ref[...], wn_ref[...], preferred_element_type=jnp.float32)
        role = (j - SGU_STEPS) // (D_NA // PROJ_TN)

        @pl.when(role == 3)
        def _gate():
            na_ref[...] = _silu(r).astype(jnp.bfloat16)

        @pl.when(role != 3)
        def _qkv():
            qscale = jnp.where(role == 0, NA_HEAD_DIM ** -0.5, 1.0).astype(jnp.float32)
            na_ref[...] = (r * qscale).astype(jnp.bfloat16)


def _in_projection(x2, mod_l, w_in_bf, norm_g, norm_b, ws_bf, bs_full, seq_len):
    m = x2.shape[0]
    blocks_per_batch = seq_len // PROJ_TM
    sgu_j = lambda j: jnp.minimum(j, SGU_STEPS - 1)
    na_j = lambda j: jnp.maximum(j - SGU_STEPS, 0)
    groups_per_step = PROJ_TN // SGU_GROUP
    n_v = D_SGU // PROJ_TN
    return pl.pallas_call(
        _proj_kernel,
        grid=(m // PROJ_TM, SGU_STEPS + NA_STEPS),
        in_specs=[
            pl.BlockSpec((PROJ_TM, D_MODEL), lambda i, j: (i, 0)),
            pl.BlockSpec((1, 1, 3 * D_MODEL), lambda i, j: (i // blocks_per_batch, 0, 0)),
            pl.BlockSpec((D_MODEL, PROJ_TN), lambda i, j: (0, sgu_j(j))),
            pl.BlockSpec((D_MODEL, PROJ_TN), lambda i, j: (0, n_v + sgu_j(j))),
            pl.BlockSpec((D_MODEL, PROJ_TN), lambda i, j: (0, 2 * n_v + sgu_j(j))),
            pl.BlockSpec((D_MODEL, PROJ_TN), lambda i, j: (0, 3 * n_v + na_j(j))),
            pl.BlockSpec((1, PROJ_TN), lambda i, j: (0, sgu_j(j))),
            pl.BlockSpec((1, PROJ_TN), lambda i, j: (0, sgu_j(j))),
            pl.BlockSpec((groups_per_step, SGU_CHUNK, SGU_CHUNK), lambda i, j: (sgu_j(j), 0, 0)),
            pl.BlockSpec((groups_per_step, SGU_CHUNK, SGU_GROUP), lambda i, j: (sgu_j(j), 0, 0)),
        ],
        out_specs=[
            pl.BlockSpec((PROJ_TM, PROJ_TN), lambda i, j: (i, sgu_j(j))),
            pl.BlockSpec((PROJ_TM, PROJ_TN), lambda i, j: (i, na_j(j))),
        ],
        out_shape=[
            jax.ShapeDtypeStruct((m, D_SGU), jnp.bfloat16),
            jax.ShapeDtypeStruct((m, 4 * D_NA), jnp.bfloat16),
        ],
        scratch_shapes=[
            pltpu.VMEM((PROJ_TM, D_MODEL), jnp.bfloat16),
            pltpu.VMEM((PROJ_TM, PROJ_TN), jnp.float32),
            pltpu.VMEM((PROJ_TM, PROJ_TN), jnp.float32),
            pltpu.VMEM((PROJ_TM, PROJ_TN), jnp.float32),
        ],
        compiler_params=pltpu.CompilerParams(
            dimension_semantics=("arbitrary", "arbitrary"), vmem_limit_bytes=VMEM_LIMIT),
        name="in_projection",
    )(x2, mod_l, w_in_bf, w_in_bf, w_in_bf, w_in_bf, norm_g, norm_b, ws_bf, bs_full)


def _attn_kernel(q_ref, k_ref, v_ref, z_ref, t_ref, y_ref, s_ref, p_ref, *, rows):
    lane = lax.broadcasted_iota(jnp.int32, (GRID_W, LANES), 1)
    first_head = lane < NA_HEAD_DIM
    win = NA_KH * GRID_W
    ones = jnp.ones((win, LANES), jnp.bfloat16)
    n_groups = rows // ATTN_G

    def window(r):
        rs = jnp.clip(r - NA_KH // 2, 0, rows - NA_KH)
        return rs, pl.ds(pl.multiple_of(rs * GRID_W, GRID_W), win)

    def scores(group, slot):
        for g in range(ATTN_G):
            r = group * ATTN_G + g
            rs, k_rows = window(r)
            rho0 = rs - r + NA_KH - 1
            q = q_ref[pl.ds(pl.multiple_of(r * GRID_W, GRID_W), GRID_W), :]
            zero = jnp.zeros_like(q)
            qq = jnp.concatenate([jnp.where(first_head, q, zero), jnp.where(first_head, zero, q)], axis=0)
            s = lax.dot_general(qq, k_ref[k_rows, :], (((1,), (1,)), ((), ())),
                                preferred_element_type=jnp.float32)
            bias = jnp.concatenate(
                [jnp.concatenate([t_ref[hh, rho0 + 2 * jt] for jt in range(NA_KH // 2)], axis=1)
                 for hh in range(HEADS_PER_STEP)], axis=0)
            s_ref[slot, g] = s + bias

    def probs(slot):
        for g in range(ATTN_G):
            s = s_ref[slot, g]
            mx = jnp.max(s, axis=-1, keepdims=True)
            p_ref[slot, g] = jnp.exp(s - mx).astype(jnp.bfloat16)

    def apply(group, slot):
        for g in range(ATTN_G):
            r = group * ATTN_G + g
            _, k_rows = window(r)
            q_rows = pl.ds(pl.multiple_of(r * GRID_W, GRID_W), GRID_W)
            v_aug = jnp.concatenate([v_ref[k_rows, :], ones], axis=1)
            oa = jnp.dot(p_ref[slot, g], v_aug, preferred_element_type=jnp.float32)
            o = oa[:, :LANES] / oa[:, LANES:]
            out = jnp.where(first_head, o[:GRID_W], o[GRID_W:])
            y_ref[q_rows, :] = (out * z_ref[q_rows, :].astype(jnp.float32)).astype(jnp.bfloat16)

    def phase(t, cur, do_scores=True, do_apply=True):
        nxt = 1 - cur
        if do_scores:
            scores(t + 1, nxt)
        probs(cur)
        if do_apply:
            apply(t - 1, nxt)

    scores(0, 0)
    phase(0, 0, do_apply=False)

    def body(tt, carry):
        t = 2 * tt + 1
        phase(t, 1)
        phase(t + 1, 0)
        return carry

    lax.fori_loop(0, (n_groups - 2) // 2, body, 0)
    phase(n_groups - 1, 1, do_scores=False)
    apply(n_groups - 1, 1)


def _attention(na, table, batch, seq_len):
    rows = seq_len // GRID_W
    n_hp = NA_HEADS // HEADS_PER_STEP
    assert (rows // ATTN_G) % 2 == 0 and rows % ATTN_G == 0
    col = lambda part: (lambda b, hp: (b, part * n_hp + hp))
    return pl.pallas_call(
        functools.partial(_attn_kernel, rows=rows),
        scratch_shapes=[
            pltpu.VMEM((2, ATTN_G, HEADS_PER_STEP * GRID_W, NA_KH * GRID_W), jnp.float32),
            pltpu.VMEM((2, ATTN_G, HEADS_PER_STEP * GRID_W, NA_KH * GRID_W), jnp.bfloat16),
        ],
        grid=(batch, n_hp),
        in_specs=[
            pl.BlockSpec((seq_len, LANES), col(0)),
            pl.BlockSpec((seq_len, LANES), col(1)),
            pl.BlockSpec((seq_len, LANES), col(2)),
            pl.BlockSpec((seq_len, LANES), col(3)),
            pl.BlockSpec((HEADS_PER_STEP, RHO_TILES, GRID_W, LANES), lambda b, hp: (hp, 0, 0, 0)),
        ],
        out_specs=pl.BlockSpec((seq_len, LANES), lambda b, hp: (b, hp)),
        out_shape=jax.ShapeDtypeStruct((batch * seq_len, D_NA), jnp.bfloat16),
        compiler_params=pltpu.CompilerParams(
            dimension_semantics=("arbitrary", "arbitrary"), vmem_limit_bytes=VMEM_LIMIT),
        name="neighbourhood_attention",
    )(na, na, na, na, table)


def _bias_tables(rpb):
    qc = np.arange(GRID_W)[:, None]
    xc = np.arange(GRID_W)[None, :]
    win_start = np.clip(qc - NA_KW // 2, 0, GRID_W - NA_KW)
    valid = (xc >= win_start) & (xc < win_start + NA_KW)
    cidx = np.clip(xc - qc + NA_KW - 1, 0, 2 * NA_KW - 2)
    bt = jnp.where(jnp.asarray(valid), rpb[:, :, :, cidx], NEG_INF)
    return jnp.concatenate([bt[:, :, :-1], bt[:, :, 1:]], axis=-1)


def _out_kernel(ya_ref, yb_ref, w_ref, x_ref, mod_ref, g_ref, b_ref, o_ref):
    y = jnp.dot(ya_ref[...], w_ref[0:D_SGU, :], preferred_element_type=jnp.float32)
    y = y + jnp.dot(yb_ref[...], w_ref[D_SGU:, :], preferred_element_type=jnp.float32)
    gate1 = 1.0 + mod_ref[0, :, 2 * D_MODEL:3 * D_MODEL]
    t = DEEPNORM_ALPHA * x_ref[...] + gate1 * y
    mu = jnp.mean(t, axis=-1, keepdims=True)
    d = t - mu
    var = jnp.mean(d * d, axis=-1, keepdims=True)
    o_ref[...] = d * lax.rsqrt(var + LN_EPS) * g_ref[...] + b_ref[...]


def _out_projection(ya, yb, w_out_bf, x2, mod_l, ln_g, ln_b, seq_len):
    m = x2.shape[0]
    blocks_per_batch = seq_len // OUT_TM
    return pl.pallas_call(
        _out_kernel,
        grid=(m // OUT_TM,),
        in_specs=[
            pl.BlockSpec((OUT_TM, D_SGU), lambda i: (i, 0)),
            pl.BlockSpec((OUT_TM, D_NA), lambda i: (i, 0)),
            pl.BlockSpec((D_SGU + D_NA, D_MODEL), lambda i: (0, 0)),
            pl.BlockSpec((OUT_TM, D_MODEL), lambda i: (i, 0)),
            pl.BlockSpec((1, 1, 3 * D_MODEL), lambda i: (i // blocks_per_batch, 0, 0)),
            pl.BlockSpec((1, D_MODEL), lambda i: (0, 0)),
            pl.BlockSpec((1, D_MODEL), lambda i: (0, 0)),
        ],
        out_specs=pl.BlockSpec((OUT_TM, D_MODEL), lambda i: (i, 0)),
        out_shape=jax.ShapeDtypeStruct((m, D_MODEL), jnp.float32),
        compiler_params=pltpu.CompilerParams(
            dimension_semantics=("arbitrary",), vmem_limit_bytes=VMEM_LIMIT),
        name="out_projection",
    )(ya, yb, w_out_bf, x2, mod_l, ln_g, ln_b)


def kernel(x, c, w_ada, b_ada, w_in, sgu_norm_g, sgu_norm_b, w_spatial, b_spatial, rpb, w_out, ln_g, ln_b):
    batch, seq_len, _ = x.shape
    assert seq_len % PROJ_TM == 0 and seq_len % OUT_TM == 0 and seq_len % (GRID_W * NA_KH) == 0
    x2 = x.reshape(batch * seq_len, D_MODEL)
    c_pad = jnp.pad(c, ((0, 8 - batch), (0, 0)))
    mod = _ada_modulation(c_pad, w_ada, b_ada)[:, :batch].reshape(DEPTH, batch, 1, 3 * D_MODEL)
    tables = _bias_tables(rpb)
    for l in range(DEPTH):
        bs_full = jnp.broadcast_to(b_spatial[l][:, :, None], (SGU_GROUPS, SGU_CHUNK, SGU_GROUP))
        ya, na = _in_projection(
            x2, mod[l], w_in[l].astype(jnp.bfloat16), sgu_norm_g[l][None], sgu_norm_b[l][None],
            w_spatial[l].astype(jnp.bfloat16), bs_full, seq_len)
        yb = _attention(na, tables[l], batch, seq_len)
        x2 = _out_projection(ya, yb, w_out[l].astype(jnp.bfloat16), x2, mod[l],
                             ln_g[l][None], ln_b[l][None], seq_len)
    return x2.reshape(batch, seq_len, D_MODEL)
```

```python
import functools

import numpy as np
import jax
import jax.numpy as jnp
from jax import lax
from jax.experimental import pallas as pl
from jax.experimental.pallas import tpu as pltpu

D_MODEL = 2048
DEPTH = 2
D_SGU = 1024
D_NA = 1024
SGU_CHUNK = 128
SGU_GROUP = 128
SGU_GROUPS = D_SGU // SGU_GROUP
NA_HEAD_DIM = 64
NA_HEADS = D_NA // NA_HEAD_DIM
GRID_W = 64
NA_KH = 8
NA_KW = 16
D_IN = 3 * D_SGU + 4 * D_NA
DEEPNORM_ALPHA = (2 * DEPTH) ** 0.25
LN_EPS = 1e-5
NEG_INF = -1e30

LANES = 128
MXU_N = 256

ADA_TN = 1024
PROJ_TM = 1024
PROJ_TN = MXU_N
SGU_STEPS = D_SGU // PROJ_TN
NA_STEPS = 4 * D_NA // PROJ_TN
LN_ROWS = 16
LN_UNROLL = 4
OUT_TM = 512
HEADS_PER_STEP = LANES // NA_HEAD_DIM
RHO_TILES = 2 * NA_KH - 2
ATTN_G = 4
VMEM_LIMIT = 56 * 1024 * 1024


def _silu(v):
    return v * jax.nn.sigmoid(v)


def _gelu(v):
    return 0.5 * v * (1.0 + lax.erf(v * (2.0 ** -0.5)))


def _ada_kernel(c_ref, w_ref, b_ref, o_ref):
    s = _silu(c_ref[...]).astype(jnp.bfloat16)
    w = w_ref[0].astype(jnp.bfloat16)
    o_ref[0] = jnp.dot(s, w, preferred_element_type=jnp.float32) + b_ref[0]


def _ada_modulation(c_pad, w_ada, b_ada):
    rows = c_pad.shape[0]
    n_out = w_ada.shape[-1]
    return pl.pallas_call(
        _ada_kernel,
        grid=(DEPTH, n_out // ADA_TN),
        in_specs=[
            pl.BlockSpec((rows, D_MODEL), lambda l, j: (0, 0)),
            pl.BlockSpec((1, D_MODEL, ADA_TN), lambda l, j: (l, 0, j)),
            pl.BlockSpec((1, 1, ADA_TN), lambda l, j: (l, 0, j)),
        ],
        out_specs=pl.BlockSpec((1, rows, ADA_TN), lambda l, j: (l, 0, j)),
        out_shape=jax.ShapeDtypeStruct((DEPTH, rows, n_out), jnp.float32),
        compiler_params=pltpu.CompilerParams(
            dimension_semantics=("arbitrary", "arbitrary"), vmem_limit_bytes=VMEM_LIMIT),
        name="ada_modulation",
    )(c_pad, w_ada, b_ada.reshape(DEPTH, 1, n_out))


def _proj_kernel(x_ref, mod_ref, wu_ref, wv_ref, wz_ref, wn_ref, ng_ref, nb_ref, ws_ref, bs_ref,
                 ya_ref, na_ref, h_ref, uvz0_ref, uvz1_ref):
    j = pl.program_id(1)
    slots = (uvz0_ref, uvz1_ref)

    def prologue():
        shift = mod_ref[0, :, 0:D_MODEL]
        scale1 = 1.0 + mod_ref[0, :, D_MODEL:2 * D_MODEL]

        def ln_chunk(ci, carry):
            rows = pl.ds(pl.multiple_of(ci * LN_ROWS, LN_ROWS), LN_ROWS)
            xv = x_ref[rows, :]
            mu = jnp.mean(xv, axis=-1, keepdims=True)
            d = xv - mu
            var = jnp.mean(d * d, axis=-1, keepdims=True)
            hn = d * lax.rsqrt(var + LN_EPS)
            h_ref[rows, :] = (hn * scale1 + shift).astype(jnp.bfloat16)
            return carry

        lax.fori_loop(0, PROJ_TM // LN_ROWS, ln_chunk, 0, unroll=LN_UNROLL)

    def sgu_dots(slot_ref):
        h = h_ref[...]
        slot_ref[0] = jnp.dot(h, wu_ref[...], preferred_element_type=jnp.float32)
        slot_ref[1] = jnp.dot(h, wv_ref[...], preferred_element_type=jnp.float32)
        slot_ref[2] = jnp.dot(h, wz_ref[...], preferred_element_type=jnp.float32)

    def sgu_epilogue(slot_ref, step):
        for ci in range(PROJ_TM // SGU_CHUNK):
            rows = slice(ci * SGU_CHUNK, (ci + 1) * SGU_CHUNK)
            for g in range(PROJ_TN // SGU_GROUP):
                cols = slice(g * SGU_GROUP, (g + 1) * SGU_GROUP)
                grp = step * (PROJ_TN // SGU_GROUP) + g
                gcols = slice(grp * SGU_GROUP, (grp + 1) * SGU_GROUP)
                gv = _gelu(slot_ref[1, rows, cols])
                mu = jnp.mean(gv, axis=-1, keepdims=True)
                d = gv - mu
                var = jnp.mean(d * d, axis=-1, keepdims=True)
                vn = d * lax.rsqrt(var + LN_EPS) * ng_ref[:, gcols] + nb_ref[:, gcols]
                sv = jnp.dot(ws_ref[grp], vn.astype(jnp.bfloat16),
                             preferred_element_type=jnp.float32) + bs_ref[grp]
                ya = _gelu(slot_ref[0, rows, cols]) * sv * _silu(slot_ref[2, rows, cols])
                ya_ref[rows, cols] = ya.astype(jnp.bfloat16)

    def na_dot(role):
        r = jnp.dot(h_ref[...], wn_ref[...], preferred_element_type=jnp.float32)
        if role == 0:
            r = r * (NA_HEAD_DIM ** -0.5)
        elif role == 3:
            r = _silu(r)
        na_ref[...] = r.astype(jnp.bfloat16)

    for jj in range(SGU_STEPS + 1):
        @pl.when(j == jj)
        def _step(jj=jj):
            if jj == 0:
                prologue()
            if jj < SGU_STEPS:
                sgu_dots(slots[jj % 2])
            else:
                na_dot(0)
            if jj >= 1:
                sgu_epilogue(slots[(jj - 1) % 2], jj - 1)

    tiles_per_role = D_NA // PROJ_TN
    for role in range(4):
        lo = SGU_STEPS + role * tiles_per_role + (1 if role == 0 else 0)
        hi = SGU_STEPS + (role + 1) * tiles_per_role

        @pl.when((j >= lo) & (j < hi))
        def _na_step(role=role):
            na_dot(role)


def _in_projection(x2, mod_l, w_in_bf, norm_g, norm_b, ws_bf, bs_full, seq_len):
    m = x2.shape[0]
    blocks_per_batch = seq_len // PROJ_TM
    sgu_j = lambda j: jnp.minimum(j, SGU_STEPS - 1)
    lag_j = lambda j: jnp.clip(j - 1, 0, SGU_STEPS - 1)
    na_j = lambda j: jnp.maximum(j - SGU_STEPS, 0)
    n_v = D_SGU // PROJ_TN
    return pl.pallas_call(
        _proj_kernel,
        grid=(m // PROJ_TM, SGU_STEPS + NA_STEPS),
        in_specs=[
            pl.BlockSpec((PROJ_TM, D_MODEL), lambda i, j: (i, 0)),
            pl.BlockSpec((1, 1, 3 * D_MODEL), lambda i, j: (i // blocks_per_batch, 0, 0)),
            pl.BlockSpec((D_MODEL, PROJ_TN), lambda i, j: (0, sgu_j(j))),
            pl.BlockSpec((D_MODEL, PROJ_TN), lambda i, j: (0, n_v + sgu_j(j))),
            pl.BlockSpec((D_MODEL, PROJ_TN), lambda i, j: (0, 2 * n_v + sgu_j(j))),
            pl.BlockSpec((D_MODEL, PROJ_TN), lambda i, j: (0, 3 * n_v + na_j(j))),
            pl.BlockSpec((1, D_SGU), lambda i, j: (0, 0)),
            pl.BlockSpec((1, D_SGU), lambda i, j: (0, 0)),
            pl.BlockSpec((SGU_GROUPS, SGU_CHUNK, SGU_CHUNK), lambda i, j: (0, 0, 0)),
            pl.BlockSpec((SGU_GROUPS, SGU_CHUNK, SGU_GROUP), lambda i, j: (0, 0, 0)),
        ],
        out_specs=[
            pl.BlockSpec((PROJ_TM, PROJ_TN), lambda i, j: (i, lag_j(j))),
            pl.BlockSpec((PROJ_TM, PROJ_TN), lambda i, j: (i, na_j(j))),
        ],
        out_shape=[
            jax.ShapeDtypeStruct((m, D_SGU), jnp.bfloat16),
            jax.ShapeDtypeStruct((m, 4 * D_NA), jnp.bfloat16),
        ],
        scratch_shapes=[
            pltpu.VMEM((PROJ_TM, D_MODEL), jnp.bfloat16),
            pltpu.VMEM((3, PROJ_TM, PROJ_TN), jnp.float32),
            pltpu.VMEM((3, PROJ_TM, PROJ_TN), jnp.float32),
        ],
        compiler_params=pltpu.CompilerParams(
            dimension_semantics=("arbitrary", "arbitrary"), vmem_limit_bytes=VMEM_LIMIT),
        name="in_projection",
    )(x2, mod_l, w_in_bf, w_in_bf, w_in_bf, w_in_bf, norm_g, norm_b, ws_bf, bs_full)


def _attn_kernel(q_ref, k_ref, v_ref, z_ref, t_ref, y_ref, s_ref, p_ref, *, rows):
    lane = lax.broadcasted_iota(jnp.int32, (GRID_W, LANES), 1)
    first_head = lane < NA_HEAD_DIM
    win = NA_KH * GRID_W
    ones = jnp.ones((win, LANES), jnp.bfloat16)
    n_groups = rows // ATTN_G

    def window(r):
        rs = jnp.clip(r - NA_KH // 2, 0, rows - NA_KH)
        return rs, pl.ds(pl.multiple_of(rs * GRID_W, GRID_W), win)

    def scores(group, slot):
        for g in range(ATTN_G):
            r = group * ATTN_G + g
            rs, k_rows = window(r)
            rho0 = rs - r + NA_KH - 1
            q = q_ref[pl.ds(pl.multiple_of(r * GRID_W, GRID_W), GRID_W), :]
            zero = jnp.zeros_like(q)
            qq = jnp.concatenate([jnp.where(first_head, q, zero), jnp.where(first_head, zero, q)], axis=0)
            s = lax.dot_general(qq, k_ref[k_rows, :], (((1,), (1,)), ((), ())),
                                preferred_element_type=jnp.float32)
            bias = jnp.concatenate(
                [jnp.concatenate([t_ref[hh, rho0 + 2 * jt] for jt in range(NA_KH // 2)], axis=1)
                 for hh in range(HEADS_PER_STEP)], axis=0)
            s_ref[slot, g] = s + bias

    def probs(slot):
        for g in range(ATTN_G):
            s = s_ref[slot, g]
            mx = jnp.max(s, axis=-1, keepdims=True)
            p_ref[slot, g] = jnp.exp(s - mx).astype(jnp.bfloat16)

    def apply(group, slot):
        for g in range(ATTN_G):
            r = group * ATTN_G + g
            _, k_rows = window(r)
            q_rows = pl.ds(pl.multiple_of(r * GRID_W, GRID_W), GRID_W)
            v_aug = jnp.concatenate([v_ref[k_rows, :], ones], axis=1)
            oa = jnp.dot(p_ref[slot, g], v_aug, preferred_element_type=jnp.float32)
            o = oa[:, :LANES] / oa[:, LANES:]
            out = jnp.where(first_head, o[:GRID_W], o[GRID_W:])
            y_ref[q_rows, :] = (out * z_ref[q_rows, :].astype(jnp.float32)).astype(jnp.bfloat16)

    def phase(t, cur, do_scores=True, do_apply=True):
        nxt = 1 - cur
        if do_scores:
            scores(t + 1, nxt)
        probs(cur)
        if do_apply:
            apply(t - 1, nxt)

    scores(0, 0)
    phase(0, 0, do_apply=False)

    def body(tt, carry):
        t = 2 * tt + 1
        phase(t, 1)
        phase(t + 1, 0)
        return carry

    lax.fori_loop(0, (n_groups - 2) // 2, body, 0)
    phase(n_groups - 1, 1, do_scores=False)
    apply(n_groups - 1, 1)


def _attention(na, table, batch, seq_len):
    rows = seq_len // GRID_W
    n_hp = NA_HEADS // HEADS_PER_STEP
    assert (rows // ATTN_G) % 2 == 0 and rows % ATTN_G == 0
    col = lambda part: (lambda b, hp: (b, part * n_hp + hp))
    return pl.pallas_call(
        functools.partial(_attn_kernel, rows=rows),
        scratch_shapes=[
            pltpu.VMEM((2, ATTN_G, HEADS_PER_STEP * GRID_W, NA_KH * GRID_W), jnp.float32),
            pltpu.VMEM((2, ATTN_G, HEADS_PER_STEP * GRID_W, NA_KH * GRID_W), jnp.bfloat16),
        ],
        grid=(batch, n_hp),
        in_specs=[
            pl.BlockSpec((seq_len, LANES), col(0)),
            pl.BlockSpec((seq_len, LANES), col(1)),
            pl.BlockSpec((seq_len, LANES), col(2)),
            pl.BlockSpec((seq_len, LANES), col(3)),
            pl.BlockSpec((HEADS_PER_STEP, RHO_TILES, GRID_W, LANES), lambda b, hp: (hp, 0, 0, 0)),
        ],
        out_specs=pl.BlockSpec((seq_len, LANES), lambda b, hp: (b, hp)),
        out_shape=jax.ShapeDtypeStruct((batch * seq_len, D_NA), jnp.bfloat16),
        compiler_params=pltpu.CompilerParams(
            dimension_semantics=("arbitrary", "arbitrary"), vmem_limit_bytes=VMEM_LIMIT),
        name="neighbourhood_attention",
    )(na, na, na, na, table)


def _bias_tables(rpb):
    qc = np.arange(GRID_W)[:, None]
    xc = np.arange(GRID_W)[None, :]
    win_start = np.clip(qc - NA_KW // 2, 0, GRID_W - NA_KW)
    valid = (xc >= win_start) & (xc < win_start + NA_KW)
    cidx = np.clip(xc - qc + NA_KW - 1, 0, 2 * NA_KW - 2)
    bt = jnp.where(jnp.asarray(valid), rpb[:, :, :, cidx], NEG_INF)
    return jnp.concatenate([bt[:, :, :-1], bt[:, :, 1:]], axis=-1)


def _out_kernel(ya_ref, yb_ref, w_ref, x_ref, mod_ref, g_ref, b_ref, o_ref):
    y = jnp.dot(ya_ref[...], w_ref[0:D_SGU, :], preferred_element_type=jnp.float32)
    y = y + jnp.dot(yb_ref[...], w_ref[D_SGU:, :], preferred_element_type=jnp.float32)
    gate1 = 1.0 + mod_ref[0, :, 2 * D_MODEL:3 * D_MODEL]
    t = DEEPNORM_ALPHA * x_ref[...] + gate1 * y
    mu = jnp.mean(t, axis=-1, keepdims=True)
    d = t - mu
    var = jnp.mean(d * d, axis=-1, keepdims=True)
    o_ref[...] = d * lax.rsqrt(var + LN_EPS) * g_ref[...] + b_ref[...]


def _out_projection(ya, yb, w_out_bf, x2, mod_l, ln_g, ln_b, seq_len):
    m = x2.shape[0]
    blocks_per_batch = seq_len // OUT_TM
    return pl.pallas_call(
        _out_kernel,
        grid=(m // OUT_TM,),
        in_specs=[
            pl.BlockSpec((OUT_TM, D_SGU), lambda i: (i, 0)),
            pl.BlockSpec((OUT_TM, D_NA), lambda i: (i, 0)),
            pl.BlockSpec((D_SGU + D_NA, D_MODEL), lambda i: (0, 0)),
            pl.BlockSpec((OUT_TM, D_MODEL), lambda i: (i, 0)),
            pl.BlockSpec((1, 1, 3 * D_MODEL), lambda i: (i // blocks_per_batch, 0, 0)),
            pl.BlockSpec((1, D_MODEL), lambda i: (0, 0)),
            pl.BlockSpec((1, D_MODEL), lambda i: (0, 0)),
        ],
        out_specs=pl.BlockSpec((OUT_TM, D_MODEL), lambda i: (i, 0)),
        out_shape=jax.ShapeDtypeStruct((m, D_MODEL), jnp.float32),
        compiler_params=pltpu.CompilerParams(
            dimension_semantics=("arbitrary",), vmem_limit_bytes=VMEM_LIMIT),
        name="out_projection",
    )(ya, yb, w_out_bf, x2, mod_l, ln_g, ln_b)


def kernel(x, c, w_ada, b_ada, w_in, sgu_norm_g, sgu_norm_b, w_spatial, b_spatial, rpb, w_out, ln_g, ln_b):
    batch, seq_len, _ = x.shape
    assert seq_len % PROJ_TM == 0 and seq_len % OUT_TM == 0 and seq_len % (GRID_W * NA_KH) == 0
    x2 = x.reshape(batch * seq_len, D_MODEL)
    c_pad = jnp.pad(c, ((0, 8 - batch), (0, 0)))
    mod = _ada_modulation(c_pad, w_ada, b_ada)[:, :batch].reshape(DEPTH, batch, 1, 3 * D_MODEL)
    tables = _bias_tables(rpb)
    for l in range(DEPTH):
        bs_full = jnp.broadcast_to(b_spatial[l][:, :, None], (SGU_GROUPS, SGU_CHUNK, SGU_GROUP))
        ya, na = _in_projection(
            x2, mod[l], w_in[l].astype(jnp.bfloat16), sgu_norm_g[l][None], sgu_norm_b[l][None],
            w_spatial[l].astype(jnp.bfloat16), bs_full, seq_len)
        yb = _attention(na, tables[l], batch, seq_len)
        x2 = _out_projection(ya, yb, w_out[l].astype(jnp.bfloat16), x2, mod[l],
                             ln_g[l][None], ln_b[l][None], seq_len)
    return x2.reshape(batch, seq_len, D_MODEL)
```

```python
import functools

import numpy as np
import jax
import jax.numpy as jnp
from jax import lax
from jax.experimental import pallas as pl
from jax.experimental.pallas import tpu as pltpu

D_MODEL = 2048
DEPTH = 2
D_SGU = 1024
D_NA = 1024
SGU_CHUNK = 128
SGU_GROUP = 128
SGU_GROUPS = D_SGU // SGU_GROUP
NA_HEAD_DIM = 64
NA_HEADS = D_NA // NA_HEAD_DIM
GRID_W = 64
NA_KH = 8
NA_KW = 16
D_IN = 3 * D_SGU + 4 * D_NA
DEEPNORM_ALPHA = (2 * DEPTH) ** 0.25
LN_EPS = 1e-5
NEG_INF = -1e30

LANES = 128
MXU_N = 256

ADA_TN = 1024
PROJ_TM = 1024
PROJ_TN = MXU_N
SGU_STEPS = D_SGU // PROJ_TN
NA_STEPS = 4 * D_NA // PROJ_TN
LN_ROWS = 16
LN_UNROLL = 4
OUT_TM = 512
HEADS_PER_STEP = LANES // NA_HEAD_DIM
RHO_TILES = 2 * NA_KH - 2
ATTN_G = 4
VMEM_LIMIT = 56 * 1024 * 1024


def _silu(v):
    return v * jax.nn.sigmoid(v)


def _gelu(v):
    return 0.5 * v * (1.0 + lax.erf(v * (2.0 ** -0.5)))


def _ada_kernel(c_ref, w_ref, b_ref, o_ref):
    s = _silu(c_ref[...]).astype(jnp.bfloat16)
    w = w_ref[0].astype(jnp.bfloat16)
    o_ref[0] = jnp.dot(s, w, preferred_element_type=jnp.float32) + b_ref[0]


def _ada_modulation(c_pad, w_ada, b_ada):
    rows = c_pad.shape[0]
    n_out = w_ada.shape[-1]
    return pl.pallas_call(
        _ada_kernel,
        grid=(DEPTH, n_out // ADA_TN),
        in_specs=[
            pl.BlockSpec((rows, D_MODEL), lambda l, j: (0, 0)),
            pl.BlockSpec((1, D_MODEL, ADA_TN), lambda l, j: (l, 0, j)),
            pl.BlockSpec((1, 1, ADA_TN), lambda l, j: (l, 0, j)),
        ],
        out_specs=pl.BlockSpec((1, rows, ADA_TN), lambda l, j: (l, 0, j)),
        out_shape=jax.ShapeDtypeStruct((DEPTH, rows, n_out), jnp.float32),
        compiler_params=pltpu.CompilerParams(
            dimension_semantics=("arbitrary", "arbitrary"), vmem_limit_bytes=VMEM_LIMIT),
        name="ada_modulation",
    )(c_pad, w_ada, b_ada.reshape(DEPTH, 1, n_out))


def _proj_kernel(x_ref, mod_ref, wu_ref, wv_ref, wz_ref, wn_ref, ng_ref, nb_ref, ws_ref, bs_ref,
                 ya_ref, na_ref, h_ref, uvz0_ref, uvz1_ref):
    j = pl.program_id(1)
    slots = (uvz0_ref, uvz1_ref)

    def prologue():
        shift = mod_ref[:, 0:D_MODEL]
        scale1 = 1.0 + mod_ref[:, D_MODEL:2 * D_MODEL]

        def ln_chunk(ci, carry):
            rows = pl.ds(pl.multiple_of(ci * LN_ROWS, LN_ROWS), LN_ROWS)
            xv = x_ref[rows, :]
            mu = jnp.mean(xv, axis=-1, keepdims=True)
            d = xv - mu
            var = jnp.mean(d * d, axis=-1, keepdims=True)
            hn = d * lax.rsqrt(var + LN_EPS)
            h_ref[rows, :] = (hn * scale1 + shift).astype(jnp.bfloat16)
            return carry

        lax.fori_loop(0, PROJ_TM // LN_ROWS, ln_chunk, 0, unroll=LN_UNROLL)

    def sgu_dots(slot_ref):
        h = h_ref[...]
        slot_ref[0] = jnp.dot(h, wu_ref[...], preferred_element_type=jnp.float32)
        slot_ref[1] = jnp.dot(h, wv_ref[...], preferred_element_type=jnp.float32)
        slot_ref[2] = jnp.dot(h, wz_ref[...], preferred_element_type=jnp.float32)

    def sgu_epilogue(slot_ref, step):
        for ci in range(PROJ_TM // SGU_CHUNK):
            rows = slice(ci * SGU_CHUNK, (ci + 1) * SGU_CHUNK)
            for g in range(PROJ_TN // SGU_GROUP):
                cols = slice(g * SGU_GROUP, (g + 1) * SGU_GROUP)
                grp = step * (PROJ_TN // SGU_GROUP) + g
                gcols = slice(grp * SGU_GROUP, (grp + 1) * SGU_GROUP)
                gv = _gelu(slot_ref[1, rows, cols])
                mu = jnp.mean(gv, axis=-1, keepdims=True)
                d = gv - mu
                var = jnp.mean(d * d, axis=-1, keepdims=True)
                vn = d * lax.rsqrt(var + LN_EPS) * ng_ref[:, gcols] + nb_ref[:, gcols]
                sv = jnp.dot(ws_ref[grp], vn.astype(jnp.bfloat16),
                             preferred_element_type=jnp.float32) + bs_ref[grp]
                ya = _gelu(slot_ref[0, rows, cols]) * sv * _silu(slot_ref[2, rows, cols])
                ya_ref[rows, cols] = ya.astype(jnp.bfloat16)

    def na_dot(role):
        r = jnp.dot(h_ref[...], wn_ref[...], preferred_element_type=jnp.float32)
        if role == 0:
            r = r * (NA_HEAD_DIM ** -0.5)
        elif role == 3:
            r = _silu(r)
        na_ref[...] = r.astype(jnp.bfloat16)

    for jj in range(SGU_STEPS + 1):
        @pl.when(j == jj)
        def _step(jj=jj):
            if jj == 0:
                prologue()
            if jj < SGU_STEPS:
                sgu_dots(slots[jj % 2])
            else:
                na_dot(0)
            if jj >= 1:
                sgu_epilogue(slots[(jj - 1) % 2], jj - 1)

    tiles_per_role = D_NA // PROJ_TN
    for role in range(4):
        lo = SGU_STEPS + role * tiles_per_role + (1 if role == 0 else 0)
        hi = SGU_STEPS + (role + 1) * tiles_per_role

        @pl.when((j >= lo) & (j < hi))
        def _na_step(role=role):
            na_dot(role)


def _in_projection(layer, x2, mod, w_in_bf, norm_g, norm_b, ws_bf, bs_full, seq_len):
    m = x2.shape[0]
    blocks_per_batch = seq_len // PROJ_TM
    sgu_j = lambda j: jnp.minimum(j, SGU_STEPS - 1)
    lag_j = lambda j: jnp.clip(j - 1, 0, SGU_STEPS - 1)
    na_j = lambda j: jnp.maximum(j - SGU_STEPS, 0)
    n_v = D_SGU // PROJ_TN
    return pl.pallas_call(
        _proj_kernel,
        grid=(m // PROJ_TM, SGU_STEPS + NA_STEPS),
        in_specs=[
            pl.BlockSpec((PROJ_TM, D_MODEL), lambda i, j: (i, 0)),
            pl.BlockSpec((None, None, 1, 3 * D_MODEL), lambda i, j: (layer, i // blocks_per_batch, 0, 0)),
            pl.BlockSpec((None, D_MODEL, PROJ_TN), lambda i, j: (layer, 0, sgu_j(j))),
            pl.BlockSpec((None, D_MODEL, PROJ_TN), lambda i, j: (layer, 0, n_v + sgu_j(j))),
            pl.BlockSpec((None, D_MODEL, PROJ_TN), lambda i, j: (layer, 0, 2 * n_v + sgu_j(j))),
            pl.BlockSpec((None, D_MODEL, PROJ_TN), lambda i, j: (layer, 0, 3 * n_v + na_j(j))),
            pl.BlockSpec((None, 1, D_SGU), lambda i, j: (layer, 0, 0)),
            pl.BlockSpec((None, 1, D_SGU), lambda i, j: (layer, 0, 0)),
            pl.BlockSpec((None, SGU_GROUPS, SGU_CHUNK, SGU_CHUNK), lambda i, j: (layer, 0, 0, 0)),
            pl.BlockSpec((None, SGU_GROUPS, SGU_CHUNK, SGU_GROUP), lambda i, j: (layer, 0, 0, 0)),
        ],
        out_specs=[
            pl.BlockSpec((PROJ_TM, PROJ_TN), lambda i, j: (i, lag_j(j))),
            pl.BlockSpec((PROJ_TM, PROJ_TN), lambda i, j: (i, na_j(j))),
        ],
        out_shape=[
            jax.ShapeDtypeStruct((m, D_SGU), jnp.bfloat16),
            jax.ShapeDtypeStruct((m, 4 * D_NA), jnp.bfloat16),
        ],
        scratch_shapes=[
            pltpu.VMEM((PROJ_TM, D_MODEL), jnp.bfloat16),
            pltpu.VMEM((3, PROJ_TM, PROJ_TN), jnp.float32),
            pltpu.VMEM((3, PROJ_TM, PROJ_TN), jnp.float32),
        ],
        compiler_params=pltpu.CompilerParams(
            dimension_semantics=("arbitrary", "arbitrary"), vmem_limit_bytes=VMEM_LIMIT),
        name="in_projection",
    )(x2, mod, w_in_bf, w_in_bf, w_in_bf, w_in_bf, norm_g, norm_b, ws_bf, bs_full)


def _attn_kernel(q_ref, k_ref, v_ref, z_ref, t_ref, y_ref, s_ref, p_ref, *, rows):
    lane = lax.broadcasted_iota(jnp.int32, (GRID_W, LANES), 1)
    first_head = lane < NA_HEAD_DIM
    win = NA_KH * GRID_W
    ones = jnp.ones((win, LANES), jnp.bfloat16)
    n_groups = rows // ATTN_G

    def window(r):
        rs = jnp.clip(r - NA_KH // 2, 0, rows - NA_KH)
        return rs, pl.ds(pl.multiple_of(rs * GRID_W, GRID_W), win)

    def scores(group, slot):
        for g in range(ATTN_G):
            r = group * ATTN_G + g
            rs, k_rows = window(r)
            rho0 = rs - r + NA_KH - 1
            q = q_ref[pl.ds(pl.multiple_of(r * GRID_W, GRID_W), GRID_W), :]
            zero = jnp.zeros_like(q)
            qq = jnp.concatenate([jnp.where(first_head, q, zero), jnp.where(first_head, zero, q)], axis=0)
            s = lax.dot_general(qq, k_ref[k_rows, :], (((1,), (1,)), ((), ())),
                                preferred_element_type=jnp.float32)
            bias = jnp.concatenate(
                [jnp.concatenate([t_ref[hh, rho0 + 2 * jt] for jt in range(NA_KH // 2)], axis=1)
                 for hh in range(HEADS_PER_STEP)], axis=0)
            s_ref[slot, g] = s + bias

    def probs(slot):
        for g in range(ATTN_G):
            s = s_ref[slot, g]
            mx = jnp.max(s, axis=-1, keepdims=True)
            p_ref[slot, g] = jnp.exp(s - mx).astype(jnp.bfloat16)

    def apply(group, slot):
        for g in range(ATTN_G):
            r = group * ATTN_G + g
            _, k_rows = window(r)
            q_rows = pl.ds(pl.multiple_of(r * GRID_W, GRID_W), GRID_W)
            v_aug = jnp.concatenate([v_ref[k_rows, :], ones], axis=1)
            oa = jnp.dot(p_ref[slot, g], v_aug, preferred_element_type=jnp.float32)
            o = oa[:, :LANES] / oa[:, LANES:]
            out = jnp.where(first_head, o[:GRID_W], o[GRID_W:])
            y_ref[q_rows, :] = (out * z_ref[q_rows, :].astype(jnp.float32)).astype(jnp.bfloat16)

    def phase(t, cur, do_scores=True, do_apply=True):
        nxt = 1 - cur
        if do_scores:
            scores(t + 1, nxt)
        probs(cur)
        if do_apply:
            apply(t - 1, nxt)

    scores(0, 0)
    phase(0, 0, do_apply=False)

    def body(tt, carry):
        t = 2 * tt + 1
        phase(t, 1)
        phase(t + 1, 0)
        return carry

    lax.fori_loop(0, (n_groups - 2) // 2, body, 0)
    phase(n_groups - 1, 1, do_scores=False)
    apply(n_groups - 1, 1)


def _attention(layer, na, tables, batch, seq_len):
    rows = seq_len // GRID_W
    n_hp = NA_HEADS // HEADS_PER_STEP
    assert (rows // ATTN_G) % 2 == 0 and rows % ATTN_G == 0
    col = lambda part: (lambda b, hp: (b, part * n_hp + hp))
    return pl.pallas_call(
        functools.partial(_attn_kernel, rows=rows),
        scratch_shapes=[
            pltpu.VMEM((2, ATTN_G, HEADS_PER_STEP * GRID_W, NA_KH * GRID_W), jnp.float32),
            pltpu.VMEM((2, ATTN_G, HEADS_PER_STEP * GRID_W, NA_KH * GRID_W), jnp.bfloat16),
        ],
        grid=(batch, n_hp),
        in_specs=[
            pl.BlockSpec((seq_len, LANES), col(0)),
            pl.BlockSpec((seq_len, LANES), col(1)),
            pl.BlockSpec((seq_len, LANES), col(2)),
            pl.BlockSpec((seq_len, LANES), col(3)),
            pl.BlockSpec((None, HEADS_PER_STEP, RHO_TILES, GRID_W, LANES),
                         lambda b, hp: (layer, hp, 0, 0, 0)),
        ],
        out_specs=pl.BlockSpec((seq_len, LANES), lambda b, hp: (b, hp)),
        out_shape=jax.ShapeDtypeStruct((batch * seq_len, D_NA), jnp.bfloat16),
        compiler_params=pltpu.CompilerParams(
            dimension_semantics=("arbitrary", "arbitrary"), vmem_limit_bytes=VMEM_LIMIT),
        name="neighbourhood_attention",
    )(na, na, na, na, tables)


def _bias_tables(rpb):
    qc = np.arange(GRID_W)[:, None]
    xc = np.arange(GRID_W)[None, :]
    win_start = np.clip(qc - NA_KW // 2, 0, GRID_W - NA_KW)
    valid = (xc >= win_start) & (xc < win_start + NA_KW)
    cidx = np.clip(xc - qc + NA_KW - 1, 0, 2 * NA_KW - 2)
    onehot = (cidx[None] == np.arange(2 * NA_KW - 1)[:, None, None]) & valid[None]
    bt = jnp.einsum("lhrc,cqx->lhrqx", rpb, jnp.asarray(onehot, jnp.float32),
                    precision=lax.Precision.HIGHEST)
    bt = bt + jnp.asarray(np.where(valid, 0.0, NEG_INF), jnp.float32)
    return jnp.concatenate([bt[:, :, :-1], bt[:, :, 1:]], axis=-1)


def _out_kernel(ya_ref, yb_ref, w_ref, x_ref, mod_ref, g_ref, b_ref, o_ref):
    y = jnp.dot(ya_ref[...], w_ref[0:D_SGU, :], preferred_element_type=jnp.float32)
    y = y + jnp.dot(yb_ref[...], w_ref[D_SGU:, :], preferred_element_type=jnp.float32)
    gate1 = 1.0 + mod_ref[:, 2 * D_MODEL:3 * D_MODEL]
    t = DEEPNORM_ALPHA * x_ref[...] + gate1 * y
    mu = jnp.mean(t, axis=-1, keepdims=True)
    d = t - mu
    var = jnp.mean(d * d, axis=-1, keepdims=True)
    o_ref[...] = d * lax.rsqrt(var + LN_EPS) * g_ref[...] + b_ref[...]


def _out_projection(layer, ya, yb, w_out_bf, x2, mod, ln_g, ln_b, seq_len):
    m = x2.shape[0]
    blocks_per_batch = seq_len // OUT_TM
    return pl.pallas_call(
        _out_kernel,
        grid=(m // OUT_TM,),
        in_specs=[
            pl.BlockSpec((OUT_TM, D_SGU), lambda i: (i, 0)),
            pl.BlockSpec((OUT_TM, D_NA), lambda i: (i, 0)),
            pl.BlockSpec((None, D_SGU + D_NA, D_MODEL), lambda i: (layer, 0, 0)),
            pl.BlockSpec((OUT_TM, D_MODEL), lambda i: (i, 0)),
            pl.BlockSpec((None, None, 1, 3 * D_MODEL), lambda i: (layer, i // blocks_per_batch, 0, 0)),
            pl.BlockSpec((None, 1, D_MODEL), lambda i: (layer, 0, 0)),
            pl.BlockSpec((None, 1, D_MODEL), lambda i: (layer, 0, 0)),
        ],
        out_specs=pl.BlockSpec((OUT_TM, D_MODEL), lambda i: (i, 0)),
        out_shape=jax.ShapeDtypeStruct((m, D_MODEL), jnp.float32),
        compiler_params=pltpu.CompilerParams(
            dimension_semantics=("arbitrary",), vmem_limit_bytes=VMEM_LIMIT),
        name="out_projection",
    )(ya, yb, w_out_bf, x2, mod, ln_g, ln_b)


def kernel(x, c, w_ada, b_ada, w_in, sgu_norm_g, sgu_norm_b, w_spatial, b_spatial, rpb, w_out, ln_g, ln_b):
    batch, seq_len, _ = x.shape
    assert seq_len % PROJ_TM == 0 and seq_len % OUT_TM == 0 and seq_len % (GRID_W * NA_KH) == 0
    x2 = x.reshape(batch * seq_len, D_MODEL)
    c_pad = jnp.pad(c, ((0, 8 - batch), (0, 0)))
    mod = _ada_modulation(c_pad, w_ada, b_ada).reshape(DEPTH, c_pad.shape[0], 1, 3 * D_MODEL)
    tables = _bias_tables(rpb)
    w_in_bf = w_in.astype(jnp.bfloat16)
    w_out_bf = w_out.astype(jnp.bfloat16)
    ws_bf = w_spatial.astype(jnp.bfloat16)
    bs_full = jnp.broadcast_to(b_spatial[..., None], (DEPTH, SGU_GROUPS, SGU_CHUNK, SGU_GROUP))
    norm_g = sgu_norm_g.reshape(DEPTH, 1, D_SGU)
    norm_b = sgu_norm_b.reshape(DEPTH, 1, D_SGU)
    ln_g3 = ln_g.reshape(DEPTH, 1, D_MODEL)
    ln_b3 = ln_b.reshape(DEPTH, 1, D_MODEL)
    for l in range(DEPTH):
        ya, na = _in_projection(l, x2, mod, w_in_bf, norm_g, norm_b, ws_bf, bs_full, seq_len)
        yb = _attention(l, na, tables, batch, seq_len)
        x2 = _out_projection(l, ya, yb, w_out_bf, x2, mod, ln_g3, ln_b3, seq_len)
    return x2.reshape(batch, seq_len, D_MODEL)
```

```python
import functools

import numpy as np
import jax
import jax.numpy as jnp
from jax import lax
from jax.experimental import pallas as pl
from jax.experimental.pallas import tpu as pltpu

D_MODEL = 2048
DEPTH = 2
D_SGU = 1024
D_NA = 1024
SGU_CHUNK = 128
SGU_GROUP = 128
SGU_GROUPS = D_SGU // SGU_GROUP
NA_HEAD_DIM = 64
NA_HEADS = D_NA // NA_HEAD_DIM
GRID_W = 64
NA_KH = 8
NA_KW = 16
D_IN = 3 * D_SGU + 4 * D_NA
DEEPNORM_ALPHA = (2 * DEPTH) ** 0.25
LN_EPS = 1e-5
NEG_INF = -1e30

LANES = 128
MXU_N = 256

ADA_TN = 1024
PROJ_TM = 1024
PROJ_TN = MXU_N
SGU_STEPS = D_SGU // PROJ_TN
NA_TN = 2 * MXU_N
NA_TILES_PER_ROLE = D_NA // NA_TN
NA_STEPS = 4 * NA_TILES_PER_ROLE
LN_ROWS = 16
LN_UNROLL = 4
LN_STEPS = 4
LN_STEP_ROWS = PROJ_TM // LN_STEPS
LN_FIRST_STEP = SGU_STEPS + NA_STEPS - LN_STEPS
OUT_TM = 512
HEADS_PER_STEP = LANES // NA_HEAD_DIM
RHO_TILES = 2 * NA_KH - 2
ATTN_G = 4
VMEM_LIMIT = 56 * 1024 * 1024


def _silu(v):
    return v * jax.nn.sigmoid(v)


def _gelu(v):
    return 0.5 * v * (1.0 + lax.erf(v * (2.0 ** -0.5)))


def _ada_kernel(c_ref, w_ref, b_ref, o_ref):
    s = _silu(c_ref[...]).astype(jnp.bfloat16)
    w = w_ref[0].astype(jnp.bfloat16)
    o_ref[0] = jnp.dot(s, w, preferred_element_type=jnp.float32) + b_ref[0]


def _ada_modulation(c_pad, w_ada, b_ada):
    rows = c_pad.shape[0]
    n_out = w_ada.shape[-1]
    return pl.pallas_call(
        _ada_kernel,
        grid=(DEPTH, n_out // ADA_TN),
        in_specs=[
            pl.BlockSpec((rows, D_MODEL), lambda l, j: (0, 0)),
            pl.BlockSpec((1, D_MODEL, ADA_TN), lambda l, j: (l, 0, j)),
            pl.BlockSpec((1, 1, ADA_TN), lambda l, j: (l, 0, j)),
        ],
        out_specs=pl.BlockSpec((1, rows, ADA_TN), lambda l, j: (l, 0, j)),
        out_shape=jax.ShapeDtypeStruct((DEPTH, rows, n_out), jnp.float32),
        compiler_params=pltpu.CompilerParams(
            dimension_semantics=("arbitrary", "arbitrary"), vmem_limit_bytes=VMEM_LIMIT),
        name="ada_modulation",
    )(c_pad, w_ada, b_ada.reshape(DEPTH, 1, n_out))


def _proj_kernel(x_ref, mod_ref, modn_ref, wu_ref, wv_ref, wz_ref, wn_ref, ng_ref, nb_ref, ws_ref, bs_ref,
                 ya_ref, na_ref, h_ref, uvz0_ref, uvz1_ref):
    i = pl.program_id(0)
    j = pl.program_id(1)
    slots = (uvz0_ref, uvz1_ref)

    def layer_norm_rows(rows, dst_ref, m_ref):
        xv = x_ref[rows, :]
        mu = jnp.mean(xv, axis=-1, keepdims=True)
        d = xv - mu
        var = jnp.mean(d * d, axis=-1, keepdims=True)
        hn = d * lax.rsqrt(var + LN_EPS)
        dst_ref[rows, :] = (hn * (1.0 + m_ref[:, D_MODEL:2 * D_MODEL]) + m_ref[:, 0:D_MODEL]).astype(jnp.bfloat16)

    @pl.when((i == 0) & (j == 0))
    def _first_block():
        def ln_chunk(ci, carry):
            layer_norm_rows(pl.ds(pl.multiple_of(ci * LN_ROWS, LN_ROWS), LN_ROWS), h_ref.at[0], mod_ref)
            return carry

        lax.fori_loop(0, PROJ_TM // LN_ROWS, ln_chunk, 0, unroll=LN_UNROLL)

    def sgu_dots(slot_ref, h):
        slot_ref[0] = jnp.dot(h, wu_ref[...], preferred_element_type=jnp.float32)
        slot_ref[1] = jnp.dot(h, wv_ref[...], preferred_element_type=jnp.float32)
        slot_ref[2] = jnp.dot(h, wz_ref[...], preferred_element_type=jnp.float32)

    def sgu_epilogue(slot_ref, step):
        for ci in range(PROJ_TM // SGU_CHUNK):
            rows = slice(ci * SGU_CHUNK, (ci + 1) * SGU_CHUNK)
            for g in range(PROJ_TN // SGU_GROUP):
                cols = slice(g * SGU_GROUP, (g + 1) * SGU_GROUP)
                grp = step * (PROJ_TN // SGU_GROUP) + g
                gcols = slice(grp * SGU_GROUP, (grp + 1) * SGU_GROUP)
                gv = _gelu(slot_ref[1, rows, cols])
                mu = jnp.mean(gv, axis=-1, keepdims=True)
                d = gv - mu
                var = jnp.mean(d * d, axis=-1, keepdims=True)
                vn = d * lax.rsqrt(var + LN_EPS) * ng_ref[:, gcols] + nb_ref[:, gcols]
                sv = jnp.dot(ws_ref[grp], vn.astype(jnp.bfloat16),
                             preferred_element_type=jnp.float32) + bs_ref[grp]
                ya = _gelu(slot_ref[0, rows, cols]) * sv * _silu(slot_ref[2, rows, cols])
                ya_ref[rows, cols] = ya.astype(jnp.bfloat16)

    def na_dot(role, h):
        r = jnp.dot(h, wn_ref[...], preferred_element_type=jnp.float32)
        if role == 0:
            r = r * (NA_HEAD_DIM ** -0.5)
        elif role == 3:
            r = _silu(r)
        na_ref[...] = r.astype(jnp.bfloat16)

    cur = i % 2
    for jj in range(SGU_STEPS + 1):
        @pl.when(j == jj)
        def _step(jj=jj):
            if jj < SGU_STEPS:
                sgu_dots(slots[jj % 2], h_ref[cur])
            else:
                na_dot(0, h_ref[cur])
            if jj >= 1:
                sgu_epilogue(slots[(jj - 1) % 2], jj - 1)

    for step in range(SGU_STEPS + 1, SGU_STEPS + NA_STEPS):
        role = (step - SGU_STEPS) // NA_TILES_PER_ROLE
        if step < LN_FIRST_STEP:
            @pl.when(j == step)
            def _na_step(role=role):
                na_dot(role, h_ref[cur])
        else:
            for parity in range(2):
                @pl.when((j == step) & (cur == parity))
                def _na_ln_step(role=role, parity=parity, step=step):
                    na_dot(role, h_ref[parity])
                    row0 = (step - LN_FIRST_STEP) * LN_STEP_ROWS
                    for ci in range(LN_STEP_ROWS // LN_ROWS):
                        rows = slice(row0 + ci * LN_ROWS, row0 + (ci + 1) * LN_ROWS)
                        layer_norm_rows(rows, h_ref.at[1 - parity], modn_ref)


def _in_projection(layer, x2, mod, w_in_bf, norm_g, norm_b, ws_bf, bs_full, seq_len):
    m = x2.shape[0]
    n_blocks = m // PROJ_TM
    blocks_per_batch = seq_len // PROJ_TM
    sgu_j = lambda j: jnp.minimum(j, SGU_STEPS - 1)
    lag_j = lambda j: jnp.clip(j - 1, 0, SGU_STEPS - 1)
    na_j = lambda j: jnp.maximum(j - SGU_STEPS, 0)
    ahead = lambda i, j: jnp.minimum(i + (j >= LN_FIRST_STEP).astype(jnp.int32), n_blocks - 1)
    nxt = lambda i: jnp.minimum(i + 1, n_blocks - 1)
    n_v = D_SGU // PROJ_TN
    return pl.pallas_call(
        _proj_kernel,
        grid=(n_blocks, SGU_STEPS + NA_STEPS),
        in_specs=[
            pl.BlockSpec((PROJ_TM, D_MODEL), lambda i, j: (ahead(i, j), 0)),
            pl.BlockSpec((None, None, 1, 3 * D_MODEL), lambda i, j: (layer, i // blocks_per_batch, 0, 0)),
            pl.BlockSpec((None, None, 1, 3 * D_MODEL), lambda i, j: (layer, nxt(i) // blocks_per_batch, 0, 0)),
            pl.BlockSpec((None, D_MODEL, PROJ_TN), lambda i, j: (layer, 0, sgu_j(j))),
            pl.BlockSpec((None, D_MODEL, PROJ_TN), lambda i, j: (layer, 0, n_v + sgu_j(j))),
            pl.BlockSpec((None, D_MODEL, PROJ_TN), lambda i, j: (layer, 0, 2 * n_v + sgu_j(j))),
            pl.BlockSpec((None, D_MODEL, NA_TN), lambda i, j: (layer, 0, 3 * D_SGU // NA_TN + na_j(j))),
            pl.BlockSpec((None, 1, D_SGU), lambda i, j: (layer, 0, 0)),
            pl.BlockSpec((None, 1, D_SGU), lambda i, j: (layer, 0, 0)),
            pl.BlockSpec((None, SGU_GROUPS, SGU_CHUNK, SGU_CHUNK), lambda i, j: (layer, 0, 0, 0)),
            pl.BlockSpec((None, SGU_GROUPS, SGU_CHUNK, SGU_GROUP), lambda i, j: (layer, 0, 0, 0)),
        ],
        out_specs=[
            pl.BlockSpec((PROJ_TM, PROJ_TN), lambda i, j: (i, lag_j(j))),
            pl.BlockSpec((PROJ_TM, NA_TN), lambda i, j: (i, na_j(j))),
        ],
        out_shape=[
            jax.ShapeDtypeStruct((m, D_SGU), jnp.bfloat16),
            jax.ShapeDtypeStruct((m, 4 * D_NA), jnp.bfloat16),
        ],
        scratch_shapes=[
            pltpu.VMEM((2, PROJ_TM, D_MODEL), jnp.bfloat16),
            pltpu.VMEM((3, PROJ_TM, PROJ_TN), jnp.float32),
            pltpu.VMEM((3, PROJ_TM, PROJ_TN), jnp.float32),
        ],
        compiler_params=pltpu.CompilerParams(
            dimension_semantics=("arbitrary", "arbitrary"), vmem_limit_bytes=VMEM_LIMIT),
        name="in_projection",
    )(x2, mod, mod, w_in_bf, w_in_bf, w_in_bf, w_in_bf, norm_g, norm_b, ws_bf, bs_full)


def _attn_kernel(q_ref, k_ref, v_ref, z_ref, t_ref, y_ref, s_ref, p_ref, *, rows):
    lane = lax.broadcasted_iota(jnp.int32, (GRID_W, LANES), 1)
    first_head = lane < NA_HEAD_DIM
    win = NA_KH * GRID_W
    ones = jnp.ones((win, LANES), jnp.bfloat16)
    n_groups = rows // ATTN_G

    def window(r):
        rs = jnp.clip(r - NA_KH // 2, 0, rows - NA_KH)
        return rs, pl.ds(pl.multiple_of(rs * GRID_W, GRID_W), win)

    def scores(group, slot):
        for g in range(ATTN_G):
            r = group * ATTN_G + g
            rs, k_rows = window(r)
            rho0 = rs - r + NA_KH - 1
            q = q_ref[pl.ds(pl.multiple_of(r * GRID_W, GRID_W), GRID_W), :]
            zero = jnp.zeros_like(q)
            qq = jnp.concatenate([jnp.where(first_head, q, zero), jnp.where(first_head, zero, q)], axis=0)
            s = lax.dot_general(qq, k_ref[k_rows, :], (((1,), (1,)), ((), ())),
                                preferred_element_type=jnp.float32)
            bias = jnp.concatenate(
                [jnp.concatenate([t_ref[hh, rho0 + 2 * jt] for jt in range(NA_KH // 2)], axis=1)
                 for hh in range(HEADS_PER_STEP)], axis=0)
            s_ref[slot, g] = s + bias

    def probs(slot):
        for g in range(ATTN_G):
            s = s_ref[slot, g]
            mx = jnp.max(s, axis=-1, keepdims=True)
            p_ref[slot, g] = jnp.exp(s - mx).astype(jnp.bfloat16)

    def apply(group, slot):
        for g in range(ATTN_G):
            r = group * ATTN_G + g
            _, k_rows = window(r)
            q_rows = pl.ds(pl.multiple_of(r * GRID_W, GRID_W), GRID_W)
            v_aug = jnp.concatenate([v_ref[k_rows, :], ones], axis=1)
            oa = jnp.dot(p_ref[slot, g], v_aug, preferred_element_type=jnp.float32)
            o = oa[:, :LANES] / oa[:, LANES:]
            out = jnp.where(first_head, o[:GRID_W], o[GRID_W:])
            y_ref[q_rows, :] = (out * z_ref[q_rows, :].astype(jnp.float32)).astype(jnp.bfloat16)

    def phase(t, cur, do_scores=True, do_apply=True):
        nxt = 1 - cur
        if do_scores:
            scores(t + 1, nxt)
        probs(cur)
        if do_apply:
            apply(t - 1, nxt)

    scores(0, 0)
    phase(0, 0, do_apply=False)

    def body(tt, carry):
        t = 2 * tt + 1
        phase(t, 1)
        phase(t + 1, 0)
        return carry

    lax.fori_loop(0, (n_groups - 2) // 2, body, 0)
    phase(n_groups - 1, 1, do_scores=False)
    apply(n_groups - 1, 1)


def _attention(layer, na, tables, batch, seq_len):
    rows = seq_len // GRID_W
    n_hp = NA_HEADS // HEADS_PER_STEP
    assert (rows // ATTN_G) % 2 == 0 and rows % ATTN_G == 0
    col = lambda part: (lambda b, hp: (b, part * n_hp + hp))
    return pl.pallas_call(
        functools.partial(_attn_kernel, rows=rows),
        scratch_shapes=[
            pltpu.VMEM((2, ATTN_G, HEADS_PER_STEP * GRID_W, NA_KH * GRID_W), jnp.float32),
            pltpu.VMEM((2, ATTN_G, HEADS_PER_STEP * GRID_W, NA_KH * GRID_W), jnp.bfloat16),
        ],
        grid=(batch, n_hp),
        in_specs=[
            pl.BlockSpec((seq_len, LANES), col(0)),
            pl.BlockSpec((seq_len, LANES), col(1)),
            pl.BlockSpec((seq_len, LANES), col(2)),
            pl.BlockSpec((seq_len, LANES), col(3)),
            pl.BlockSpec((None, HEADS_PER_STEP, RHO_TILES, GRID_W, LANES),
                         lambda b, hp: (layer, hp, 0, 0, 0)),
        ],
        out_specs=pl.BlockSpec((seq_len, LANES), lambda b, hp: (b, hp)),
        out_shape=jax.ShapeDtypeStruct((batch * seq_len, D_NA), jnp.bfloat16),
        compiler_params=pltpu.CompilerParams(
            dimension_semantics=("arbitrary", "arbitrary"), vmem_limit_bytes=VMEM_LIMIT),
        name="neighbourhood_attention",
    )(na, na, na, na, tables)


def _bias_tables(rpb):
    qc = np.arange(GRID_W)[:, None]
    xc = np.arange(GRID_W)[None, :]
    win_start = np.clip(qc - NA_KW // 2, 0, GRID_W - NA_KW)
    valid = (xc >= win_start) & (xc < win_start + NA_KW)
    cidx = np.clip(xc - qc + NA_KW - 1, 0, 2 * NA_KW - 2)
    onehot = (cidx[None] == np.arange(2 * NA_KW - 1)[:, None, None]) & valid[None]
    bt = jnp.einsum("lhrc,cqx->lhrqx", rpb, jnp.asarray(onehot, jnp.float32),
                    precision=lax.Precision.HIGHEST)
    bt = bt + jnp.asarray(np.where(valid, 0.0, NEG_INF), jnp.float32)
    return jnp.concatenate([bt[:, :, :-1], bt[:, :, 1:]], axis=-1)


def _out_kernel(ya_ref, yb_ref, w_ref, x_ref, mod_ref, g_ref, b_ref, o_ref):
    y = jnp.dot(ya_ref[...], w_ref[0:D_SGU, :], preferred_element_type=jnp.float32)
    y = y + jnp.dot(yb_ref[...], w_ref[D_SGU:, :], preferred_element_type=jnp.float32)
    gate1 = 1.0 + mod_ref[:, 2 * D_MODEL:3 * D_MODEL]
    t = DEEPNORM_ALPHA * x_ref[...] + gate1 * y
    mu = jnp.mean(t, axis=-1, keepdims=True)
    d = t - mu
    var = jnp.mean(d * d, axis=-1, keepdims=True)
    o_ref[...] = d * lax.rsqrt(var + LN_EPS) * g_ref[...] + b_ref[...]


def _out_projection(layer, ya, yb, w_out_bf, x2, mod, ln_g, ln_b, seq_len):
    m = x2.shape[0]
    blocks_per_batch = seq_len // OUT_TM
    return pl.pallas_call(
        _out_kernel,
        grid=(m // OUT_TM,),
        in_specs=[
            pl.BlockSpec((OUT_TM, D_SGU), lambda i: (i, 0)),
            pl.BlockSpec((OUT_TM, D_NA), lambda i: (i, 0)),
            pl.BlockSpec((None, D_SGU + D_NA, D_MODEL), lambda i: (layer, 0, 0)),
            pl.BlockSpec((OUT_TM, D_MODEL), lambda i: (i, 0)),
            pl.BlockSpec((None, None, 1, 3 * D_MODEL), lambda i: (layer, i // blocks_per_batch, 0, 0)),
            pl.BlockSpec((None, 1, D_MODEL), lambda i: (layer, 0, 0)),
            pl.BlockSpec((None, 1, D_MODEL), lambda i: (layer, 0, 0)),
        ],
        out_specs=pl.BlockSpec((OUT_TM, D_MODEL), lambda i: (i, 0)),
        out_shape=jax.ShapeDtypeStruct((m, D_MODEL), jnp.float32),
        compiler_params=pltpu.CompilerParams(
            dimension_semantics=("arbitrary",), vmem_limit_bytes=VMEM_LIMIT),
        name="out_projection",
    )(ya, yb, w_out_bf, x2, mod, ln_g, ln_b)


def kernel(x, c, w_ada, b_ada, w_in, sgu_norm_g, sgu_norm_b, w_spatial, b_spatial, rpb, w_out, ln_g, ln_b):
    batch, seq_len, _ = x.shape
    assert seq_len % PROJ_TM == 0 and seq_len % OUT_TM == 0 and seq_len % (GRID_W * NA_KH) == 0
    x2 = x.reshape(batch * seq_len, D_MODEL)
    c_pad = jnp.pad(c, ((0, 8 - batch), (0, 0)))
    mod = _ada_modulation(c_pad, w_ada, b_ada).reshape(DEPTH, c_pad.shape[0], 1, 3 * D_MODEL)
    tables = _bias_tables(rpb)
    w_in_bf = w_in.astype(jnp.bfloat16)
    w_out_bf = w_out.astype(jnp.bfloat16)
    ws_bf = w_spatial.astype(jnp.bfloat16)
    bs_full = jnp.broadcast_to(b_spatial[..., None], (DEPTH, SGU_GROUPS, SGU_CHUNK, SGU_GROUP))
    norm_g = sgu_norm_g.reshape(DEPTH, 1, D_SGU)
    norm_b = sgu_norm_b.reshape(DEPTH, 1, D_SGU)
    ln_g3 = ln_g.reshape(DEPTH, 1, D_MODEL)
    ln_b3 = ln_b.reshape(DEPTH, 1, D_MODEL)
    for l in range(DEPTH):
        ya, na = _in_projection(l, x2, mod, w_in_bf, norm_g, norm_b, ws_bf, bs_full, seq_len)
        yb = _attention(l, na, tables, batch, seq_len)
        x2 = _out_projection(l, ya, yb, w_out_bf, x2, mod, ln_g3, ln_b3, seq_len)
    return x2.reshape(batch, seq_len, D_MODEL)
```

```python
import functools

import numpy as np
import jax
import jax.numpy as jnp
from jax import lax
from jax.experimental import pallas as pl
from jax.experimental.pallas import tpu as pltpu

D_MODEL = 2048
DEPTH = 2
D_SGU = 1024
D_NA = 1024
SGU_CHUNK = 128
SGU_GROUP = 128
SGU_GROUPS = D_SGU // SGU_GROUP
NA_HEAD_DIM = 64
NA_HEADS = D_NA // NA_HEAD_DIM
GRID_W = 64
NA_KH = 8
NA_KW = 16
D_IN = 3 * D_SGU + 4 * D_NA
DEEPNORM_ALPHA = (2 * DEPTH) ** 0.25
LN_EPS = 1e-5
NEG_INF = -1e30

LANES = 128
MXU_N = 256

ADA_TN = 1024
PROJ_TM = 1024
PROJ_TN = MXU_N
SGU_STEPS = D_SGU // PROJ_TN
NA_TN = 2 * MXU_N
NA_TILES_PER_ROLE = D_NA // NA_TN
NA_STEPS = 4 * NA_TILES_PER_ROLE
LN_ROWS = 16
LN_UNROLL = 4
NA_SPLIT = 4
NA_PIECE_ROWS = PROJ_TM // NA_SPLIT
LN_STEPS = 4
LN_STEP_ROWS = PROJ_TM // LN_STEPS
LN_FIRST_STEP = SGU_STEPS + NA_STEPS - LN_STEPS
OUT_TM = 512
OUT_SPLIT = 2
HEADS_PER_STEP = LANES // NA_HEAD_DIM
RHO_TILES = 2 * NA_KH - 2
ATTN_G = 4
VMEM_LIMIT = 56 * 1024 * 1024


def _silu(v):
    return v * jax.nn.sigmoid(v)


def _gelu(v):
    return 0.5 * v * (1.0 + lax.erf(v * (2.0 ** -0.5)))


def _ada_kernel(c_ref, w_ref, b_ref, o_ref):
    s = _silu(c_ref[...]).astype(jnp.bfloat16)
    w = w_ref[0].astype(jnp.bfloat16)
    o_ref[0] = jnp.dot(s, w, preferred_element_type=jnp.float32) + b_ref[0]


def _ada_modulation(c_pad, w_ada, b_ada):
    rows = c_pad.shape[0]
    n_out = w_ada.shape[-1]
    return pl.pallas_call(
        _ada_kernel,
        grid=(DEPTH, n_out // ADA_TN),
        in_specs=[
            pl.BlockSpec((rows, D_MODEL), lambda l, j: (0, 0)),
            pl.BlockSpec((1, D_MODEL, ADA_TN), lambda l, j: (l, 0, j)),
            pl.BlockSpec((1, 1, ADA_TN), lambda l, j: (l, 0, j)),
        ],
        out_specs=pl.BlockSpec((1, rows, ADA_TN), lambda l, j: (l, 0, j)),
        out_shape=jax.ShapeDtypeStruct((DEPTH, rows, n_out), jnp.float32),
        compiler_params=pltpu.CompilerParams(
            dimension_semantics=("arbitrary", "arbitrary"), vmem_limit_bytes=VMEM_LIMIT),
        name="ada_modulation",
    )(c_pad, w_ada, b_ada.reshape(DEPTH, 1, n_out))


def _proj_kernel(x_ref, mod_ref, modn_ref, wu_ref, wv_ref, wz_ref, wn_ref, ng_ref, nb_ref, ws_ref, bs_ref,
                 ya_ref, na_ref, h_ref, uvz0_ref, uvz1_ref):
    i = pl.program_id(0)
    j = pl.program_id(1)
    slots = (uvz0_ref, uvz1_ref)

    def layer_norm_rows(rows, dst_ref, m_ref):
        xv = x_ref[rows, :]
        mu = jnp.mean(xv, axis=-1, keepdims=True)
        d = xv - mu
        var = jnp.mean(d * d, axis=-1, keepdims=True)
        hn = d * lax.rsqrt(var + LN_EPS)
        dst_ref[rows, :] = (hn * (1.0 + m_ref[:, D_MODEL:2 * D_MODEL]) + m_ref[:, 0:D_MODEL]).astype(jnp.bfloat16)

    @pl.when((i == 0) & (j == 0))
    def _first_block():
        def ln_chunk(ci, carry):
            layer_norm_rows(pl.ds(pl.multiple_of(ci * LN_ROWS, LN_ROWS), LN_ROWS), h_ref.at[0], mod_ref)
            return carry

        lax.fori_loop(0, PROJ_TM // LN_ROWS, ln_chunk, 0, unroll=LN_UNROLL)

    def sgu_dots(slot_ref, h):
        slot_ref[0] = jnp.dot(h, wu_ref[...], preferred_element_type=jnp.float32)
        slot_ref[1] = jnp.dot(h, wv_ref[...], preferred_element_type=jnp.float32)
        slot_ref[2] = jnp.dot(h, wz_ref[...], preferred_element_type=jnp.float32)

    def sgu_epilogue(slot_ref, step):
        for ci in range(PROJ_TM // SGU_CHUNK):
            rows = slice(ci * SGU_CHUNK, (ci + 1) * SGU_CHUNK)
            for g in range(PROJ_TN // SGU_GROUP):
                cols = slice(g * SGU_GROUP, (g + 1) * SGU_GROUP)
                grp = step * (PROJ_TN // SGU_GROUP) + g
                gcols = slice(grp * SGU_GROUP, (grp + 1) * SGU_GROUP)
                gv = _gelu(slot_ref[1, rows, cols])
                mu = jnp.mean(gv, axis=-1, keepdims=True)
                d = gv - mu
                var = jnp.mean(d * d, axis=-1, keepdims=True)
                vn = d * lax.rsqrt(var + LN_EPS) * ng_ref[:, gcols] + nb_ref[:, gcols]
                sv = jnp.dot(ws_ref[grp], vn.astype(jnp.bfloat16),
                             preferred_element_type=jnp.float32) + bs_ref[grp]
                ya = _gelu(slot_ref[0, rows, cols]) * sv * _silu(slot_ref[2, rows, cols])
                ya_ref[rows, cols] = ya.astype(jnp.bfloat16)

    def na_dot(role, slab, extra=None):
        w = wn_ref[...]
        for piece in range(NA_SPLIT):
            rows = slice(piece * NA_PIECE_ROWS, (piece + 1) * NA_PIECE_ROWS)
            r = jnp.dot(h_ref[slab, rows, :], w, preferred_element_type=jnp.float32)
            if role == 0:
                r = r * (NA_HEAD_DIM ** -0.5)
            elif role == 3:
                r = _silu(r)
            na_ref[rows, :] = r.astype(jnp.bfloat16)
            if extra is not None:
                extra(piece)

    cur = i % 2
    for jj in range(SGU_STEPS + 1):
        @pl.when(j == jj)
        def _step(jj=jj):
            if jj < SGU_STEPS:
                sgu_dots(slots[jj % 2], h_ref[cur])
            else:
                na_dot(0, cur)
            if jj >= 1:
                sgu_epilogue(slots[(jj - 1) % 2], jj - 1)

    for step in range(SGU_STEPS + 1, SGU_STEPS + NA_STEPS):
        role = (step - SGU_STEPS) // NA_TILES_PER_ROLE
        if step < LN_FIRST_STEP:
            @pl.when(j == step)
            def _na_step(role=role):
                na_dot(role, cur)
        else:
            for parity in range(2):
                @pl.when((j == step) & (cur == parity))
                def _na_ln_step(role=role, parity=parity, step=step):
                    row0 = (step - LN_FIRST_STEP) * LN_STEP_ROWS
                    chunks_per_piece = LN_STEP_ROWS // LN_ROWS // NA_SPLIT

                    def ln_piece(piece):
                        for ci in range(piece * chunks_per_piece, (piece + 1) * chunks_per_piece):
                            rows = slice(row0 + ci * LN_ROWS, row0 + (ci + 1) * LN_ROWS)
                            layer_norm_rows(rows, h_ref.at[1 - parity], modn_ref)

                    na_dot(role, parity, extra=ln_piece)


def _in_projection(layer, x2, mod, w_in_bf, norm_g, norm_b, ws_bf, bs_full, seq_len):
    m = x2.shape[0]
    n_blocks = m // PROJ_TM
    blocks_per_batch = seq_len // PROJ_TM
    sgu_j = lambda j: jnp.minimum(j, SGU_STEPS - 1)
    lag_j = lambda j: jnp.clip(j - 1, 0, SGU_STEPS - 1)
    na_j = lambda j: jnp.maximum(j - SGU_STEPS, 0)
    ahead = lambda i, j: jnp.minimum(i + (j >= LN_FIRST_STEP).astype(jnp.int32), n_blocks - 1)
    nxt = lambda i: jnp.minimum(i + 1, n_blocks - 1)
    n_v = D_SGU // PROJ_TN
    return pl.pallas_call(
        _proj_kernel,
        grid=(n_blocks, SGU_STEPS + NA_STEPS),
        in_specs=[
            pl.BlockSpec((PROJ_TM, D_MODEL), lambda i, j: (ahead(i, j), 0)),
            pl.BlockSpec((None, None, 1, 3 * D_MODEL), lambda i, j: (layer, i // blocks_per_batch, 0, 0)),
            pl.BlockSpec((None, None, 1, 3 * D_MODEL), lambda i, j: (layer, nxt(i) // blocks_per_batch, 0, 0)),
            pl.BlockSpec((None, D_MODEL, PROJ_TN), lambda i, j: (layer, 0, sgu_j(j))),
            pl.BlockSpec((None, D_MODEL, PROJ_TN), lambda i, j: (layer, 0, n_v + sgu_j(j))),
            pl.BlockSpec((None, D_MODEL, PROJ_TN), lambda i, j: (layer, 0, 2 * n_v + sgu_j(j))),
            pl.BlockSpec((None, D_MODEL, NA_TN), lambda i, j: (layer, 0, 3 * D_SGU // NA_TN + na_j(j))),
            pl.BlockSpec((None, 1, D_SGU), lambda i, j: (layer, 0, 0)),
            pl.BlockSpec((None, 1, D_SGU), lambda i, j: (layer, 0, 0)),
            pl.BlockSpec((None, SGU_GROUPS, SGU_CHUNK, SGU_CHUNK), lambda i, j: (layer, 0, 0, 0)),
            pl.BlockSpec((None, SGU_GROUPS, SGU_CHUNK, SGU_GROUP), lambda i, j: (layer, 0, 0, 0)),
        ],
        out_specs=[
            pl.BlockSpec((PROJ_TM, PROJ_TN), lambda i, j: (i, lag_j(j))),
            pl.BlockSpec((PROJ_TM, NA_TN), lambda i, j: (i, na_j(j))),
        ],
        out_shape=[
            jax.ShapeDtypeStruct((m, D_SGU), jnp.bfloat16),
            jax.ShapeDtypeStruct((m, 4 * D_NA), jnp.bfloat16),
        ],
        scratch_shapes=[
            pltpu.VMEM((2, PROJ_TM, D_MODEL), jnp.bfloat16),
            pltpu.VMEM((3, PROJ_TM, PROJ_TN), jnp.float32),
            pltpu.VMEM((3, PROJ_TM, PROJ_TN), jnp.float32),
        ],
        compiler_params=pltpu.CompilerParams(
            dimension_semantics=("arbitrary", "arbitrary"), vmem_limit_bytes=VMEM_LIMIT),
        name="in_projection",
    )(x2, mod, mod, w_in_bf, w_in_bf, w_in_bf, w_in_bf, norm_g, norm_b, ws_bf, bs_full)


def _attn_kernel(q_ref, k_ref, v_ref, z_ref, t_ref, y_ref, s_ref, p_ref, *, rows):
    lane = lax.broadcasted_iota(jnp.int32, (GRID_W, LANES), 1)
    first_head = lane < NA_HEAD_DIM
    win = NA_KH * GRID_W
    ones = jnp.ones((win, LANES), jnp.bfloat16)
    n_groups = rows // ATTN_G

    def window(r):
        rs = jnp.clip(r - NA_KH // 2, 0, rows - NA_KH)
        return rs, pl.ds(pl.multiple_of(rs * GRID_W, GRID_W), win)

    def scores(group, slot):
        for g in range(ATTN_G):
            r = group * ATTN_G + g
            rs, k_rows = window(r)
            rho0 = rs - r + NA_KH - 1
            q = q_ref[pl.ds(pl.multiple_of(r * GRID_W, GRID_W), GRID_W), :]
            zero = jnp.zeros_like(q)
            qq = jnp.concatenate([jnp.where(first_head, q, zero), jnp.where(first_head, zero, q)], axis=0)
            s = lax.dot_general(qq, k_ref[k_rows, :], (((1,), (1,)), ((), ())),
                                preferred_element_type=jnp.float32)
            bias = jnp.concatenate(
                [jnp.concatenate([t_ref[hh, rho0 + 2 * jt] for jt in range(NA_KH // 2)], axis=1)
                 for hh in range(HEADS_PER_STEP)], axis=0)
            s_ref[slot, g] = s + bias

    def probs(slot):
        for g in range(ATTN_G):
            s = s_ref[slot, g]
            mx = jnp.max(s, axis=-1, keepdims=True)
            p_ref[slot, g] = jnp.exp(s - mx).astype(jnp.bfloat16)

    def apply(group, slot):
        for g in range(ATTN_G):
            r = group * ATTN_G + g
            _, k_rows = window(r)
            q_rows = pl.ds(pl.multiple_of(r * GRID_W, GRID_W), GRID_W)
            v_aug = jnp.concatenate([v_ref[k_rows, :], ones], axis=1)
            oa = jnp.dot(p_ref[slot, g], v_aug, preferred_element_type=jnp.float32)
            o = oa[:, :LANES] / oa[:, LANES:]
            out = jnp.where(first_head, o[:GRID_W], o[GRID_W:])
            y_ref[q_rows, :] = (out * z_ref[q_rows, :].astype(jnp.float32)).astype(jnp.bfloat16)

    def phase(t, cur, do_scores=True, do_apply=True):
        nxt = 1 - cur
        if do_scores:
            scores(t + 1, nxt)
        probs(cur)
        if do_apply:
            apply(t - 1, nxt)

    scores(0, 0)
    phase(0, 0, do_apply=False)

    def body(tt, carry):
        t = 2 * tt + 1
        phase(t, 1)
        phase(t + 1, 0)
        return carry

    lax.fori_loop(0, (n_groups - 2) // 2, body, 0)
    phase(n_groups - 1, 1, do_scores=False)
    apply(n_groups - 1, 1)


def _attention(layer, na, tables, batch, seq_len):
    rows = seq_len // GRID_W
    n_hp = NA_HEADS // HEADS_PER_STEP
    assert (rows // ATTN_G) % 2 == 0 and rows % ATTN_G == 0
    col = lambda part: (lambda b, hp: (b, part * n_hp + hp))
    return pl.pallas_call(
        functools.partial(_attn_kernel, rows=rows),
        scratch_shapes=[
            pltpu.VMEM((2, ATTN_G, HEADS_PER_STEP * GRID_W, NA_KH * GRID_W), jnp.float32),
            pltpu.VMEM((2, ATTN_G, HEADS_PER_STEP * GRID_W, NA_KH * GRID_W), jnp.bfloat16),
        ],
        grid=(batch, n_hp),
        in_specs=[
            pl.BlockSpec((seq_len, LANES), col(0)),
            pl.BlockSpec((seq_len, LANES), col(1)),
            pl.BlockSpec((seq_len, LANES), col(2)),
            pl.BlockSpec((seq_len, LANES), col(3)),
            pl.BlockSpec((None, HEADS_PER_STEP, RHO_TILES, GRID_W, LANES),
                         lambda b, hp: (layer, hp, 0, 0, 0)),
        ],
        out_specs=pl.BlockSpec((seq_len, LANES), lambda b, hp: (b, hp)),
        out_shape=jax.ShapeDtypeStruct((batch * seq_len, D_NA), jnp.bfloat16),
        compiler_params=pltpu.CompilerParams(
            dimension_semantics=("arbitrary", "arbitrary"), vmem_limit_bytes=VMEM_LIMIT),
        name="neighbourhood_attention",
    )(na, na, na, na, tables)


def _bias_tables(rpb):
    qc = np.arange(GRID_W)[:, None]
    xc = np.arange(GRID_W)[None, :]
    win_start = np.clip(qc - NA_KW // 2, 0, GRID_W - NA_KW)
    valid = (xc >= win_start) & (xc < win_start + NA_KW)
    cidx = np.clip(xc - qc + NA_KW - 1, 0, 2 * NA_KW - 2)
    onehot = (cidx[None] == np.arange(2 * NA_KW - 1)[:, None, None]) & valid[None]
    bt = jnp.einsum("lhrc,cqx->lhrqx", rpb, jnp.asarray(onehot, jnp.float32),
                    precision=lax.Precision.HIGHEST)
    bt = bt + jnp.asarray(np.where(valid, 0.0, NEG_INF), jnp.float32)
    return jnp.concatenate([bt[:, :, :-1], bt[:, :, 1:]], axis=-1)


def _out_kernel(ya_ref, yb_ref, w_ref, x_ref, mod_ref, g_ref, b_ref, o_ref):
    gate1 = 1.0 + mod_ref[:, 2 * D_MODEL:3 * D_MODEL]
    for piece in range(OUT_SPLIT):
        rows = slice(piece * (OUT_TM // OUT_SPLIT), (piece + 1) * (OUT_TM // OUT_SPLIT))
        y = jnp.dot(ya_ref[rows, :], w_ref[0:D_SGU, :], preferred_element_type=jnp.float32)
        y = y + jnp.dot(yb_ref[rows, :], w_ref[D_SGU:, :], preferred_element_type=jnp.float32)
        t = DEEPNORM_ALPHA * x_ref[rows, :] + gate1 * y
        mu = jnp.mean(t, axis=-1, keepdims=True)
        d = t - mu
        var = jnp.mean(d * d, axis=-1, keepdims=True)
        o_ref[rows, :] = d * lax.rsqrt(var + LN_EPS) * g_ref[...] + b_ref[...]


def _out_projection(layer, ya, yb, w_out_bf, x2, mod, ln_g, ln_b, seq_len):
    m = x2.shape[0]
    blocks_per_batch = seq_len // OUT_TM
    return pl.pallas_call(
        _out_kernel,
        grid=(m // OUT_TM,),
        in_specs=[
            pl.BlockSpec((OUT_TM, D_SGU), lambda i: (i, 0)),
            pl.BlockSpec((OUT_TM, D_NA), lambda i: (i, 0)),
            pl.BlockSpec((None, D_SGU + D_NA, D_MODEL), lambda i: (layer, 0, 0)),
            pl.BlockSpec((OUT_TM, D_MODEL), lambda i: (i, 0)),
            pl.BlockSpec((None, None, 1, 3 * D_MODEL), lambda i: (layer, i // blocks_per_batch, 0, 0)),
            pl.BlockSpec((None, 1, D_MODEL), lambda i: (layer, 0, 0)),
            pl.BlockSpec((None, 1, D_MODEL), lambda i: (layer, 0, 0)),
        ],
        out_specs=pl.BlockSpec((OUT_TM, D_MODEL), lambda i: (i, 0)),
        out_shape=jax.ShapeDtypeStruct((m, D_MODEL), jnp.float32),
        compiler_params=pltpu.CompilerParams(
            dimension_semantics=("arbitrary",), vmem_limit_bytes=VMEM_LIMIT),
        name="out_projection",
    )(ya, yb, w_out_bf, x2, mod, ln_g, ln_b)


def kernel(x, c, w_ada, b_ada, w_in, sgu_norm_g, sgu_norm_b, w_spatial, b_spatial, rpb, w_out, ln_g, ln_b):
    batch, seq_len, _ = x.shape
    assert seq_len % PROJ_TM == 0 and seq_len % OUT_TM == 0 and seq_len % (GRID_W * NA_KH) == 0
    x2 = x.reshape(batch * seq_len, D_MODEL)
    c_pad = jnp.pad(c, ((0, 8 - batch), (0, 0)))
    mod = _ada_modulation(c_pad, w_ada, b_ada).reshape(DEPTH, c_pad.shape[0], 1, 3 * D_MODEL)
    tables = _bias_tables(rpb)
    w_in_bf = w_in.astype(jnp.bfloat16)
    w_out_bf = w_out.astype(jnp.bfloat16)
    ws_bf = w_spatial.astype(jnp.bfloat16)
    bs_full = jnp.broadcast_to(b_spatial[..., None], (DEPTH, SGU_GROUPS, SGU_CHUNK, SGU_GROUP))
    norm_g = sgu_norm_g.reshape(DEPTH, 1, D_SGU)
    norm_b = sgu_norm_b.reshape(DEPTH, 1, D_SGU)
    ln_g3 = ln_g.reshape(DEPTH, 1, D_MODEL)
    ln_b3 = ln_b.reshape(DEPTH, 1, D_MODEL)
    for l in range(DEPTH):
        ya, na = _in_projection(l, x2, mod, w_in_bf, norm_g, norm_b, ws_bf, bs_full, seq_len)
        yb = _attention(l, na, tables, batch, seq_len)
        x2 = _out_projection(l, ya, yb, w_out_bf, x2, mod, ln_g3, ln_b3, seq_len)
    return x2.reshape(batch, seq_len, D_MODEL)
```

```python
import functools

import numpy as np
import jax
import jax.numpy as jnp
from jax import lax
from jax.experimental import pallas as pl
from jax.experimental.pallas import tpu as pltpu

D_MODEL = 2048
DEPTH = 2
D_SGU = 1024
D_NA = 1024
SGU_CHUNK = 128
SGU_GROUP = 128
SGU_GROUPS = D_SGU // SGU_GROUP
NA_HEAD_DIM = 64
NA_HEADS = D_NA // NA_HEAD_DIM
GRID_W = 64
NA_KH = 8
NA_KW = 16
D_IN = 3 * D_SGU + 4 * D_NA
DEEPNORM_ALPHA = (2 * DEPTH) ** 0.25
LN_EPS = 1e-5
NEG_INF = -1e30

LANES = 128
MXU_N = 256

ADA_TN = 1024
PROJ_TM = 256
PROJ_TN = MXU_N
SGU_STEPS = D_SGU // PROJ_TN
NA_TN = 2 * MXU_N
NA_TILES_PER_ROLE = D_NA // NA_TN
NA_STEPS = 4 * NA_TILES_PER_ROLE
LN_ROWS = 16
LN_UNROLL = 4
OUT_TM = 512
OUT_SPLIT = 2
HEADS_PER_STEP = LANES // NA_HEAD_DIM
RHO_TILES = 2 * NA_KH - 2
ATTN_G = 4
VMEM_LIMIT = 56 * 1024 * 1024


def _silu(v):
    return v * jax.nn.sigmoid(v)


def _gelu(v):
    return 0.5 * v * (1.0 + lax.erf(v * (2.0 ** -0.5)))


def _ada_kernel(c_ref, w_ref, b_ref, o_ref):
    s = _silu(c_ref[...]).astype(jnp.bfloat16)
    w = w_ref[0].astype(jnp.bfloat16)
    o_ref[0] = jnp.dot(s, w, preferred_element_type=jnp.float32) + b_ref[0]


def _ada_modulation(c_pad, w_ada, b_ada):
    rows = c_pad.shape[0]
    n_out = w_ada.shape[-1]
    return pl.pallas_call(
        _ada_kernel,
        grid=(DEPTH, n_out // ADA_TN),
        in_specs=[
            pl.BlockSpec((rows, D_MODEL), lambda l, j: (0, 0)),
            pl.BlockSpec((1, D_MODEL, ADA_TN), lambda l, j: (l, 0, j)),
            pl.BlockSpec((1, 1, ADA_TN), lambda l, j: (l, 0, j)),
        ],
        out_specs=pl.BlockSpec((1, rows, ADA_TN), lambda l, j: (l, 0, j)),
        out_shape=jax.ShapeDtypeStruct((DEPTH, rows, n_out), jnp.float32),
        compiler_params=pltpu.CompilerParams(
            dimension_semantics=("arbitrary", "arbitrary"), vmem_limit_bytes=VMEM_LIMIT),
        name="ada_modulation",
    )(c_pad, w_ada, b_ada.reshape(DEPTH, 1, n_out))


def _proj_kernel(xf_ref, xn_ref, modf_ref, modn_ref, w_ref, ng_ref, nb_ref, ws_ref, bs_ref,
                 ya_ref, na_ref, h_ref):
    i = pl.program_id(0)

    def layer_norm_rows(src_ref, rows, dst_ref, m_ref):
        xv = src_ref[rows, :]
        mu = jnp.mean(xv, axis=-1, keepdims=True)
        d = xv - mu
        var = jnp.mean(d * d, axis=-1, keepdims=True)
        hn = d * lax.rsqrt(var + LN_EPS)
        dst_ref[rows, :] = (hn * (1.0 + m_ref[:, D_MODEL:2 * D_MODEL]) + m_ref[:, 0:D_MODEL]).astype(jnp.bfloat16)

    @pl.when(i == 0)
    def _first_block():
        def ln_chunk(ci, carry):
            rows = pl.ds(pl.multiple_of(ci * LN_ROWS, LN_ROWS), LN_ROWS)
            layer_norm_rows(xf_ref, rows, h_ref.at[0], modf_ref)
            return carry

        lax.fori_loop(0, PROJ_TM // LN_ROWS, ln_chunk, 0, unroll=LN_UNROLL)

    def sgu_group(h, step):
        def cols_of(part):
            c0 = part * D_SGU + step * PROJ_TN
            return slice(c0, c0 + PROJ_TN)

        u = jnp.dot(h, w_ref[:, cols_of(0)], preferred_element_type=jnp.float32)
        v = jnp.dot(h, w_ref[:, cols_of(1)], preferred_element_type=jnp.float32)
        z = jnp.dot(h, w_ref[:, cols_of(2)], preferred_element_type=jnp.float32)
        for ci in range(PROJ_TM // SGU_CHUNK):
            rows = slice(ci * SGU_CHUNK, (ci + 1) * SGU_CHUNK)
            for g in range(PROJ_TN // SGU_GROUP):
                cols = slice(g * SGU_GROUP, (g + 1) * SGU_GROUP)
                grp = step * (PROJ_TN // SGU_GROUP) + g
                gcols = slice(grp * SGU_GROUP, (grp + 1) * SGU_GROUP)
                gv = _gelu(v[rows, cols])
                mu = jnp.mean(gv, axis=-1, keepdims=True)
                d = gv - mu
                var = jnp.mean(d * d, axis=-1, keepdims=True)
                vn = d * lax.rsqrt(var + LN_EPS) * ng_ref[:, gcols] + nb_ref[:, gcols]
                sv = jnp.dot(ws_ref[grp], vn.astype(jnp.bfloat16),
                             preferred_element_type=jnp.float32) + bs_ref[grp]
                ya = _gelu(u[rows, cols]) * sv * _silu(z[rows, cols])
                ya_ref[rows, gcols] = ya.astype(jnp.bfloat16)

    def na_tile(h, tile):
        role = tile // NA_TILES_PER_ROLE
        cols = slice(tile * NA_TN, (tile + 1) * NA_TN)
        wcols = slice(3 * D_SGU + tile * NA_TN, 3 * D_SGU + (tile + 1) * NA_TN)
        r = jnp.dot(h, w_ref[:, wcols], preferred_element_type=jnp.float32)
        if role == 0:
            r = r * (NA_HEAD_DIM ** -0.5)
        elif role == 3:
            r = _silu(r)
        na_ref[:, cols] = r.astype(jnp.bfloat16)

    n_units = SGU_STEPS + NA_STEPS
    ln_chunks = PROJ_TM // LN_ROWS
    for parity in range(2):
        @pl.when(i % 2 == parity)
        def _block(parity=parity):
            h = h_ref[parity]
            done = 0
            for unit in range(n_units):
                if unit < SGU_STEPS:
                    sgu_group(h, unit)
                else:
                    na_tile(h, unit - SGU_STEPS)
                upto = (unit + 1) * ln_chunks // n_units
                for ci in range(done, upto):
                    rows = slice(ci * LN_ROWS, (ci + 1) * LN_ROWS)
                    layer_norm_rows(xn_ref, rows, h_ref.at[1 - parity], modn_ref)
                done = upto


def _in_projection(layer, x2, mod, w_in_bf, norm_g, norm_b, ws_bf, bs_full, seq_len):
    m = x2.shape[0]
    n_blocks = m // PROJ_TM
    blocks_per_batch = seq_len // PROJ_TM
    nxt = lambda i: jnp.minimum(i + 1, n_blocks - 1)
    resident = pl.Buffered(1)
    return pl.pallas_call(
        _proj_kernel,
        grid=(n_blocks,),
        in_specs=[
            pl.BlockSpec((PROJ_TM, D_MODEL), lambda i: (0, 0), pipeline_mode=resident),
            pl.BlockSpec((PROJ_TM, D_MODEL), lambda i: (nxt(i), 0)),
            pl.BlockSpec((None, None, 1, 3 * D_MODEL), lambda i: (layer, 0, 0, 0)),
            pl.BlockSpec((None, None, 1, 3 * D_MODEL), lambda i: (layer, nxt(i) // blocks_per_batch, 0, 0)),
            pl.BlockSpec((None, D_MODEL, D_IN), lambda i: (layer, 0, 0), pipeline_mode=resident),
            pl.BlockSpec((None, 1, D_SGU), lambda i: (layer, 0, 0)),
            pl.BlockSpec((None, 1, D_SGU), lambda i: (layer, 0, 0)),
            pl.BlockSpec((None, SGU_GROUPS, SGU_CHUNK, SGU_CHUNK), lambda i: (layer, 0, 0, 0)),
            pl.BlockSpec((None, SGU_GROUPS, SGU_CHUNK, SGU_GROUP), lambda i: (layer, 0, 0, 0)),
        ],
        out_specs=[
            pl.BlockSpec((PROJ_TM, D_SGU), lambda i: (i, 0)),
            pl.BlockSpec((PROJ_TM, 4 * D_NA), lambda i: (i, 0)),
        ],
        out_shape=[
            jax.ShapeDtypeStruct((m, D_SGU), jnp.bfloat16),
            jax.ShapeDtypeStruct((m, 4 * D_NA), jnp.bfloat16),
        ],
        scratch_shapes=[pltpu.VMEM((2, PROJ_TM, D_MODEL), jnp.bfloat16)],
        compiler_params=pltpu.CompilerParams(
            dimension_semantics=("arbitrary",), vmem_limit_bytes=VMEM_LIMIT),
        name="in_projection",
    )(x2, x2, mod, mod, w_in_bf, norm_g, norm_b, ws_bf, bs_full)


def _attn_kernel(q_ref, k_ref, v_ref, z_ref, t_ref, y_ref, s_ref, p_ref, *, rows):
    lane = lax.broadcasted_iota(jnp.int32, (GRID_W, LANES), 1)
    first_head = lane < NA_HEAD_DIM
    win = NA_KH * GRID_W
    ones = jnp.ones((win, LANES), jnp.bfloat16)
    n_groups = rows // ATTN_G

    def window(r):
        rs = jnp.clip(r - NA_KH // 2, 0, rows - NA_KH)
        return rs, pl.ds(pl.multiple_of(rs * GRID_W, GRID_W), win)

    def scores(group, slot):
        for g in range(ATTN_G):
            r = group * ATTN_G + g
            rs, k_rows = window(r)
            rho0 = rs - r + NA_KH - 1
            q = q_ref[pl.ds(pl.multiple_of(r * GRID_W, GRID_W), GRID_W), :]
            zero = jnp.zeros_like(q)
            qq = jnp.concatenate([jnp.where(first_head, q, zero), jnp.where(first_head, zero, q)], axis=0)
            s = lax.dot_general(qq, k_ref[k_rows, :], (((1,), (1,)), ((), ())),
                                preferred_element_type=jnp.float32)
            bias = jnp.concatenate(
                [jnp.concatenate([t_ref[hh, rho0 + 2 * jt] for jt in range(NA_KH // 2)], axis=1)
                 for hh in range(HEADS_PER_STEP)], axis=0)
            s_ref[slot, g] = s + bias

    def probs(slot):
        for g in range(ATTN_G):
            s = s_ref[slot, g]
            mx = jnp.max(s, axis=-1, keepdims=True)
            p_ref[slot, g] = jnp.exp(s - mx).astype(jnp.bfloat16)

    def apply(group, slot):
        for g in range(ATTN_G):
            r = group * ATTN_G + g
            _, k_rows = window(r)
            q_rows = pl.ds(pl.multiple_of(r * GRID_W, GRID_W), GRID_W)
            v_aug = jnp.concatenate([v_ref[k_rows, :], ones], axis=1)
            oa = jnp.dot(p_ref[slot, g], v_aug, preferred_element_type=jnp.float32)
            o = oa[:, :LANES] / oa[:, LANES:]
            out = jnp.where(first_head, o[:GRID_W], o[GRID_W:])
            y_ref[q_rows, :] = (out * z_ref[q_rows, :].astype(jnp.float32)).astype(jnp.bfloat16)

    def phase(t, cur, do_scores=True, do_apply=True):
        nxt = 1 - cur
        if do_scores:
            scores(t + 1, nxt)
        probs(cur)
        if do_apply:
            apply(t - 1, nxt)

    scores(0, 0)
    phase(0, 0, do_apply=False)

    def body(tt, carry):
        t = 2 * tt + 1
        phase(t, 1)
        phase(t + 1, 0)
        return carry

    lax.fori_loop(0, (n_groups - 2) // 2, body, 0)
    phase(n_groups - 1, 1, do_scores=False)
    apply(n_groups - 1, 1)


def _attention(layer, na, tables, batch, seq_len):
    rows = seq_len // GRID_W
    n_hp = NA_HEADS // HEADS_PER_STEP
    assert (rows // ATTN_G) % 2 == 0 and rows % ATTN_G == 0
    col = lambda part: (lambda b, hp: (b, part * n_hp + hp))
    return pl.pallas_call(
        functools.partial(_attn_kernel, rows=rows),
        scratch_shapes=[
            pltpu.VMEM((2, ATTN_G, HEADS_PER_STEP * GRID_W, NA_KH * GRID_W), jnp.float32),
            pltpu.VMEM((2, ATTN_G, HEADS_PER_STEP * GRID_W, NA_KH * GRID_W), jnp.bfloat16),
        ],
        grid=(batch, n_hp),
        in_specs=[
            pl.BlockSpec((seq_len, LANES), col(0)),
            pl.BlockSpec((seq_len, LANES), col(1)),
            pl.BlockSpec((seq_len, LANES), col(2)),
            pl.BlockSpec((seq_len, LANES), col(3)),
            pl.BlockSpec((None, HEADS_PER_STEP, RHO_TILES, GRID_W, LANES),
                         lambda b, hp: (layer, hp, 0, 0, 0)),
        ],
        out_specs=pl.BlockSpec((seq_len, LANES), lambda b, hp: (b, hp)),
        out_shape=jax.ShapeDtypeStruct((batch * seq_len, D_NA), jnp.bfloat16),
        compiler_params=pltpu.CompilerParams(
            dimension_semantics=("arbitrary", "arbitrary"), vmem_limit_bytes=VMEM_LIMIT),
        name="neighbourhood_attention",
    )(na, na, na, na, tables)


def _bias_tables(rpb):
    qc = np.arange(GRID_W)[:, None]
    xc = np.arange(GRID_W)[None, :]
    win_start = np.clip(qc - NA_KW // 2, 0, GRID_W - NA_KW)
    valid = (xc >= win_start) & (xc < win_start + NA_KW)
    cidx = np.clip(xc - qc + NA_KW - 1, 0, 2 * NA_KW - 2)
    onehot = (cidx[None] == np.arange(2 * NA_KW - 1)[:, None, None]) & valid[None]
    bt = jnp.einsum("lhrc,cqx->lhrqx", rpb, jnp.asarray(onehot, jnp.float32),
                    precision=lax.Precision.HIGHEST)
    bt = bt + jnp.asarray(np.where(valid, 0.0, NEG_INF), jnp.float32)
    return jnp.concatenate([bt[:, :, :-1], bt[:, :, 1:]], axis=-1)


def _out_kernel(ya_ref, yb_ref, w_ref, x_ref, mod_ref, g_ref, b_ref, o_ref):
    gate1 = 1.0 + mod_ref[:, 2 * D_MODEL:3 * D_MODEL]
    for piece in range(OUT_SPLIT):
        rows = slice(piece * (OUT_TM // OUT_SPLIT), (piece + 1) * (OUT_TM // OUT_SPLIT))
        y = jnp.dot(ya_ref[rows, :], w_ref[0:D_SGU, :], preferred_element_type=jnp.float32)
        y = y + jnp.dot(yb_ref[rows, :], w_ref[D_SGU:, :], preferred_element_type=jnp.float32)
        t = DEEPNORM_ALPHA * x_ref[rows, :] + gate1 * y
        mu = jnp.mean(t, axis=-1, keepdims=True)
        d = t - mu
        var = jnp.mean(d * d, axis=-1, keepdims=True)
        o_ref[rows, :] = d * lax.rsqrt(var + LN_EPS) * g_ref[...] + b_ref[...]


def _out_projection(layer, ya, yb, w_out_bf, x2, mod, ln_g, ln_b, seq_len):
    m = x2.shape[0]
    blocks_per_batch = seq_len // OUT_TM
    return pl.pallas_call(
        _out_kernel,
        grid=(m // OUT_TM,),
        in_specs=[
            pl.BlockSpec((OUT_TM, D_SGU), lambda i: (i, 0)),
            pl.BlockSpec((OUT_TM, D_NA), lambda i: (i, 0)),
            pl.BlockSpec((None, D_SGU + D_NA, D_MODEL), lambda i: (layer, 0, 0)),
            pl.BlockSpec((OUT_TM, D_MODEL), lambda i: (i, 0)),
            pl.BlockSpec((None, None, 1, 3 * D_MODEL), lambda i: (layer, i // blocks_per_batch, 0, 0)),
            pl.BlockSpec((None, 1, D_MODEL), lambda i: (layer, 0, 0)),
            pl.BlockSpec((None, 1, D_MODEL), lambda i: (layer, 0, 0)),
        ],
        out_specs=pl.BlockSpec((OUT_TM, D_MODEL), lambda i: (i, 0)),
        out_shape=jax.ShapeDtypeStruct((m, D_MODEL), jnp.float32),
        compiler_params=pltpu.CompilerParams(
            dimension_semantics=("arbitrary",), vmem_limit_bytes=VMEM_LIMIT),
        name="out_projection",
    )(ya, yb, w_out_bf, x2, mod, ln_g, ln_b)


def kernel(x, c, w_ada, b_ada, w_in, sgu_norm_g, sgu_norm_b, w_spatial, b_spatial, rpb, w_out, ln_g, ln_b):
    batch, seq_len, _ = x.shape
    assert seq_len % PROJ_TM == 0 and seq_len % OUT_TM == 0 and seq_len % (GRID_W * NA_KH) == 0
    x2 = x.reshape(batch * seq_len, D_MODEL)
    c_pad = jnp.pad(c, ((0, 8 - batch), (0, 0)))
    mod = _ada_modulation(c_pad, w_ada, b_ada).reshape(DEPTH, c_pad.shape[0], 1, 3 * D_MODEL)
    tables = _bias_tables(rpb)
    w_in_bf = w_in.astype(jnp.bfloat16)
    w_out_bf = w_out.astype(jnp.bfloat16)
    ws_bf = w_spatial.astype(jnp.bfloat16)
    bs_full = jnp.broadcast_to(b_spatial[..., None], (DEPTH, SGU_GROUPS, SGU_CHUNK, SGU_GROUP))
    norm_g = sgu_norm_g.reshape(DEPTH, 1, D_SGU)
    norm_b = sgu_norm_b.reshape(DEPTH, 1, D_SGU)
    ln_g3 = ln_g.reshape(DEPTH, 1, D_MODEL)
    ln_b3 = ln_b.reshape(DEPTH, 1, D_MODEL)
    for l in range(DEPTH):
        ya, na = _in_projection(l, x2, mod, w_in_bf, norm_g, norm_b, ws_bf, bs_full, seq_len)
        yb = _attention(l, na, tables, batch, seq_len)
        x2 = _out_projection(l, ya, yb, w_out_bf, x2, mod, ln_g3, ln_b3, seq_len)
    return x2.reshape(batch, seq_len, D_MODEL)
```

```python
import functools

import numpy as np
import jax
import jax.numpy as jnp
from jax import lax
from jax.experimental import pallas as pl
from jax.experimental.pallas import tpu as pltpu

D_MODEL = 2048
DEPTH = 2
D_SGU = 1024
D_NA = 1024
SGU_CHUNK = 128
SGU_GROUP = 128
SGU_GROUPS = D_SGU // SGU_GROUP
NA_HEAD_DIM = 64
NA_HEADS = D_NA // NA_HEAD_DIM
GRID_W = 64
NA_KH = 8
NA_KW = 16
D_IN = 3 * D_SGU + 4 * D_NA
DEEPNORM_ALPHA = (2 * DEPTH) ** 0.25
LN_EPS = 1e-5
NEG_INF = -1e30

LANES = 128
MXU_N = 256

ADA_TN = 1024
PROJ_TM = 256
PROJ_TN = MXU_N
SGU_STEPS = D_SGU // PROJ_TN
ROLE_Q, ROLE_K, ROLE_V, ROLE_GATE = range(4)
LN_ROWS = 16
LN_UNROLL = 4
OUT_TM = 512
OUT_SPLIT = 2
HEADS_PER_STEP = LANES // NA_HEAD_DIM
RHO_TILES = 2 * NA_KH - 2
ATTN_G = 4
VMEM_LIMIT = 56 * 1024 * 1024


def _silu(v):
    return v * jax.nn.sigmoid(v)


def _gelu(v):
    return 0.5 * v * (1.0 + lax.erf(v * (2.0 ** -0.5)))


def _ada_kernel(c_ref, w_ref, b_ref, o_ref):
    s = _silu(c_ref[...]).astype(jnp.bfloat16)
    w = w_ref[0].astype(jnp.bfloat16)
    o_ref[0] = jnp.dot(s, w, preferred_element_type=jnp.float32) + b_ref[0]


def _ada_modulation(c_pad, w_ada, b_ada):
    rows = c_pad.shape[0]
    n_out = w_ada.shape[-1]
    return pl.pallas_call(
        _ada_kernel,
        grid=(DEPTH, n_out // ADA_TN),
        in_specs=[
            pl.BlockSpec((rows, D_MODEL), lambda l, j: (0, 0)),
            pl.BlockSpec((1, D_MODEL, ADA_TN), lambda l, j: (l, 0, j)),
            pl.BlockSpec((1, 1, ADA_TN), lambda l, j: (l, 0, j)),
        ],
        out_specs=pl.BlockSpec((1, rows, ADA_TN), lambda l, j: (l, 0, j)),
        out_shape=jax.ShapeDtypeStruct((DEPTH, rows, n_out), jnp.float32),
        compiler_params=pltpu.CompilerParams(
            dimension_semantics=("arbitrary", "arbitrary"), vmem_limit_bytes=VMEM_LIMIT),
        name="ada_modulation",
    )(c_pad, w_ada, b_ada.reshape(DEPTH, 1, n_out))


def _proj_kernel(xf_ref, xn_ref, modf_ref, modn_ref, w_ref, ng_ref, nb_ref, ws_ref, bs_ref,
                 ya_ref, na_ref, h_ref):
    i = pl.program_id(0)

    def layer_norm_rows(src_ref, rows, dst_ref, m_ref):
        xv = src_ref[rows, :]
        mu = jnp.mean(xv, axis=-1, keepdims=True)
        d = xv - mu
        var = jnp.mean(d * d, axis=-1, keepdims=True)
        hn = d * lax.rsqrt(var + LN_EPS)
        dst_ref[rows, :] = (hn * (1.0 + m_ref[:, D_MODEL:2 * D_MODEL]) + m_ref[:, 0:D_MODEL]).astype(jnp.bfloat16)

    @pl.when(i == 0)
    def _first_block():
        def ln_chunk(ci, carry):
            rows = pl.ds(pl.multiple_of(ci * LN_ROWS, LN_ROWS), LN_ROWS)
            layer_norm_rows(xf_ref, rows, h_ref.at[0], modf_ref)
            return carry

        lax.fori_loop(0, PROJ_TM // LN_ROWS, ln_chunk, 0, unroll=LN_UNROLL)

    def sgu_dot(h, step):
        w = jnp.concatenate(
            [w_ref[:, part * D_SGU + step * PROJ_TN:part * D_SGU + (step + 1) * PROJ_TN] for part in range(3)],
            axis=1)
        return jnp.dot(h, w, preferred_element_type=jnp.float32)

    def sgu_epilogue(uvz, step):
        for ci in range(PROJ_TM // SGU_CHUNK):
            rows = slice(ci * SGU_CHUNK, (ci + 1) * SGU_CHUNK)
            for g in range(PROJ_TN // SGU_GROUP):
                grp = step * (PROJ_TN // SGU_GROUP) + g
                gcols = slice(grp * SGU_GROUP, (grp + 1) * SGU_GROUP)
                u, v, z = (uvz[rows, part * PROJ_TN + g * SGU_GROUP:part * PROJ_TN + (g + 1) * SGU_GROUP]
                           for part in range(3))
                gv = _gelu(v)
                mu = jnp.mean(gv, axis=-1, keepdims=True)
                d = gv - mu
                var = jnp.mean(d * d, axis=-1, keepdims=True)
                vn = d * lax.rsqrt(var + LN_EPS) * ng_ref[:, gcols] + nb_ref[:, gcols]
                sv = jnp.dot(ws_ref[grp], vn.astype(jnp.bfloat16),
                             preferred_element_type=jnp.float32) + bs_ref[grp]
                ya_ref[rows, gcols] = (_gelu(u) * sv * _silu(z)).astype(jnp.bfloat16)

    def na_dot(h, role):
        return jnp.dot(h, w_ref[:, 3 * D_SGU + role * D_NA:3 * D_SGU + (role + 1) * D_NA],
                       preferred_element_type=jnp.float32)

    def na_epilogue(r, role):
        if role == ROLE_Q:
            r = r * (NA_HEAD_DIM ** -0.5)
        elif role == ROLE_GATE:
            r = _silu(r)
        na_ref[:, role * D_NA:(role + 1) * D_NA] = r.astype(jnp.bfloat16)

    units = [("sgu", s) for s in range(SGU_STEPS)] + [("na", r) for r in (ROLE_GATE, ROLE_Q, ROLE_K, ROLE_V)]
    ln_units = len(units) - 2
    ln_chunks = PROJ_TM // LN_ROWS
    for parity in range(2):
        @pl.when(i % 2 == parity)
        def _block(parity=parity):
            h = h_ref[parity]

            def unit_dot(n):
                kind, idx = units[n]
                return sgu_dot(h, idx) if kind == "sgu" else na_dot(h, idx)

            done = 0
            acc = unit_dot(0)
            for n, (kind, idx) in enumerate(units):
                nxt_acc = unit_dot(n + 1) if n + 1 < len(units) else None
                if kind == "sgu":
                    sgu_epilogue(acc, idx)
                else:
                    na_epilogue(acc, idx)
                acc = nxt_acc
                upto = min(ln_chunks, (n + 1) * ln_chunks // ln_units)
                for ci in range(done, upto):
                    rows = slice(ci * LN_ROWS, (ci + 1) * LN_ROWS)
                    layer_norm_rows(xn_ref, rows, h_ref.at[1 - parity], modn_ref)
                done = upto


def _in_projection(layer, x2, mod, w_in_bf, norm_g, norm_b, ws_bf, bs_full, seq_len):
    m = x2.shape[0]
    n_blocks = m // PROJ_TM
    blocks_per_batch = seq_len // PROJ_TM
    nxt = lambda i: jnp.minimum(i + 1, n_blocks - 1)
    resident = pl.Buffered(1)
    return pl.pallas_call(
        _proj_kernel,
        grid=(n_blocks,),
        in_specs=[
            pl.BlockSpec((PROJ_TM, D_MODEL), lambda i: (0, 0), pipeline_mode=resident),
            pl.BlockSpec((PROJ_TM, D_MODEL), lambda i: (nxt(i), 0)),
            pl.BlockSpec((None, None, 1, 3 * D_MODEL), lambda i: (layer, 0, 0, 0)),
            pl.BlockSpec((None, None, 1, 3 * D_MODEL), lambda i: (layer, nxt(i) // blocks_per_batch, 0, 0)),
            pl.BlockSpec((None, D_MODEL, D_IN), lambda i: (layer, 0, 0), pipeline_mode=resident),
            pl.BlockSpec((None, 1, D_SGU), lambda i: (layer, 0, 0)),
            pl.BlockSpec((None, 1, D_SGU), lambda i: (layer, 0, 0)),
            pl.BlockSpec((None, SGU_GROUPS, SGU_CHUNK, SGU_CHUNK), lambda i: (layer, 0, 0, 0)),
            pl.BlockSpec((None, SGU_GROUPS, SGU_CHUNK, SGU_GROUP), lambda i: (layer, 0, 0, 0)),
        ],
        out_specs=[
            pl.BlockSpec((PROJ_TM, D_SGU), lambda i: (i, 0)),
            pl.BlockSpec((PROJ_TM, 4 * D_NA), lambda i: (i, 0)),
        ],
        out_shape=[
            jax.ShapeDtypeStruct((m, D_SGU), jnp.bfloat16),
            jax.ShapeDtypeStruct((m, 4 * D_NA), jnp.bfloat16),
        ],
        scratch_shapes=[pltpu.VMEM((2, PROJ_TM, D_MODEL), jnp.bfloat16)],
        compiler_params=pltpu.CompilerParams(
            dimension_semantics=("arbitrary",), vmem_limit_bytes=VMEM_LIMIT),
        name="in_projection",
    )(x2, x2, mod, mod, w_in_bf, norm_g, norm_b, ws_bf, bs_full)


def _attn_kernel(q_ref, k_ref, v_ref, z_ref, t_ref, y_ref, s_ref, p_ref, *, rows):
    lane = lax.broadcasted_iota(jnp.int32, (GRID_W, LANES), 1)
    first_head = lane < NA_HEAD_DIM
    win = NA_KH * GRID_W
    ones = jnp.ones((win, LANES), jnp.bfloat16)
    n_groups = rows // ATTN_G

    def window(r):
        rs = jnp.clip(r - NA_KH // 2, 0, rows - NA_KH)
        return rs, pl.ds(pl.multiple_of(rs * GRID_W, GRID_W), win)

    def scores(group, slot):
        for g in range(ATTN_G):
            r = group * ATTN_G + g
            rs, k_rows = window(r)
            rho0 = rs - r + NA_KH - 1
            q = q_ref[pl.ds(pl.multiple_of(r * GRID_W, GRID_W), GRID_W), :]
            zero = jnp.zeros_like(q)
            qq = jnp.concatenate([jnp.where(first_head, q, zero), jnp.where(first_head, zero, q)], axis=0)
            s = lax.dot_general(qq, k_ref[k_rows, :], (((1,), (1,)), ((), ())),
                                preferred_element_type=jnp.float32)
            bias = jnp.concatenate(
                [jnp.concatenate([t_ref[hh, rho0 + 2 * jt] for jt in range(NA_KH // 2)], axis=1)
                 for hh in range(HEADS_PER_STEP)], axis=0)
            s_ref[slot, g] = s + bias

    def probs(slot):
        for g in range(ATTN_G):
            s = s_ref[slot, g]
            mx = jnp.max(s, axis=-1, keepdims=True)
            p_ref[slot, g] = jnp.exp(s - mx).astype(jnp.bfloat16)

    def apply(group, slot):
        for g in range(ATTN_G):
            r = group * ATTN_G + g
            _, k_rows = window(r)
            q_rows = pl.ds(pl.multiple_of(r * GRID_W, GRID_W), GRID_W)
            v_aug = jnp.concatenate([v_ref[k_rows, :], ones], axis=1)
            oa = jnp.dot(p_ref[slot, g], v_aug, preferred_element_type=jnp.float32)
            o = oa[:, :LANES] / oa[:, LANES:]
            out = jnp.where(first_head, o[:GRID_W], o[GRID_W:])
            y_ref[q_rows, :] = (out * z_ref[q_rows, :].astype(jnp.float32)).astype(jnp.bfloat16)

    def phase(t, cur, do_scores=True, do_apply=True):
        nxt = 1 - cur
        if do_scores:
            scores(t + 1, nxt)
        probs(cur)
        if do_apply:
            apply(t - 1, nxt)

    scores(0, 0)
    phase(0, 0, do_apply=False)

    def body(tt, carry):
        t = 2 * tt + 1
        phase(t, 1)
        phase(t + 1, 0)
        return carry

    lax.fori_loop(0, (n_groups - 2) // 2, body, 0)
    phase(n_groups - 1, 1, do_scores=False)
    apply(n_groups - 1, 1)


def _attention(layer, na, tables, batch, seq_len):
    rows = seq_len // GRID_W
    n_hp = NA_HEADS // HEADS_PER_STEP
    assert (rows // ATTN_G) % 2 == 0 and rows % ATTN_G == 0
    col = lambda part: (lambda b, hp: (b, part * n_hp + hp))
    return pl.pallas_call(
        functools.partial(_attn_kernel, rows=rows),
        scratch_shapes=[
            pltpu.VMEM((2, ATTN_G, HEADS_PER_STEP * GRID_W, NA_KH * GRID_W), jnp.float32),
            pltpu.VMEM((2, ATTN_G, HEADS_PER_STEP * GRID_W, NA_KH * GRID_W), jnp.bfloat16),
        ],
        grid=(batch, n_hp),
        in_specs=[
            pl.BlockSpec((seq_len, LANES), col(0)),
            pl.BlockSpec((seq_len, LANES), col(1)),
            pl.BlockSpec((seq_len, LANES), col(2)),
            pl.BlockSpec((seq_len, LANES), col(3)),
            pl.BlockSpec((None, HEADS_PER_STEP, RHO_TILES, GRID_W, LANES),
                         lambda b, hp: (layer, hp, 0, 0, 0)),
        ],
        out_specs=pl.BlockSpec((seq_len, LANES), lambda b, hp: (b, hp)),
        out_shape=jax.ShapeDtypeStruct((batch * seq_len, D_NA), jnp.bfloat16),
        compiler_params=pltpu.CompilerParams(
            dimension_semantics=("arbitrary", "arbitrary"), vmem_limit_bytes=VMEM_LIMIT),
        name="neighbourhood_attention",
    )(na, na, na, na, tables)


def _bias_tables(rpb):
    qc = np.arange(GRID_W)[:, None]
    xc = np.arange(GRID_W)[None, :]
    win_start = np.clip(qc - NA_KW // 2, 0, GRID_W - NA_KW)
    valid = (xc >= win_start) & (xc < win_start + NA_KW)
    cidx = np.clip(xc - qc + NA_KW - 1, 0, 2 * NA_KW - 2)
    onehot = (cidx[None] == np.arange(2 * NA_KW - 1)[:, None, None]) & valid[None]
    bt = jnp.einsum("lhrc,cqx->lhrqx", rpb, jnp.asarray(onehot, jnp.float32),
                    precision=lax.Precision.HIGHEST)
    bt = bt + jnp.asarray(np.where(valid, 0.0, NEG_INF), jnp.float32)
    return jnp.concatenate([bt[:, :, :-1], bt[:, :, 1:]], axis=-1)


def _out_kernel(ya_ref, yb_ref, w_ref, x_ref, mod_ref, g_ref, b_ref, o_ref):
    gate1 = 1.0 + mod_ref[:, 2 * D_MODEL:3 * D_MODEL]
    for piece in range(OUT_SPLIT):
        rows = slice(piece * (OUT_TM // OUT_SPLIT), (piece + 1) * (OUT_TM // OUT_SPLIT))
        y = jnp.dot(ya_ref[rows, :], w_ref[0:D_SGU, :], preferred_element_type=jnp.float32)
        y = y + jnp.dot(yb_ref[rows, :], w_ref[D_SGU:, :], preferred_element_type=jnp.float32)
        t = DEEPNORM_ALPHA * x_ref[rows, :] + gate1 * y
        mu = jnp.mean(t, axis=-1, keepdims=True)
        d = t - mu
        var = jnp.mean(d * d, axis=-1, keepdims=True)
        o_ref[rows, :] = d * lax.rsqrt(var + LN_EPS) * g_ref[...] + b_ref[...]


def _out_projection(layer, ya, yb, w_out_bf, x2, mod, ln_g, ln_b, seq_len):
    m = x2.shape[0]
    blocks_per_batch = seq_len // OUT_TM
    return pl.pallas_call(
        _out_kernel,
        grid=(m // OUT_TM,),
        in_specs=[
            pl.BlockSpec((OUT_TM, D_SGU), lambda i: (i, 0)),
            pl.BlockSpec((OUT_TM, D_NA), lambda i: (i, 0)),
            pl.BlockSpec((None, D_SGU + D_NA, D_MODEL), lambda i: (layer, 0, 0)),
            pl.BlockSpec((OUT_TM, D_MODEL), lambda i: (i, 0)),
            pl.BlockSpec((None, None, 1, 3 * D_MODEL), lambda i: (layer, i // blocks_per_batch, 0, 0)),
            pl.BlockSpec((None, 1, D_MODEL), lambda i: (layer, 0, 0)),
            pl.BlockSpec((None, 1, D_MODEL), lambda i: (layer, 0, 0)),
        ],
        out_specs=pl.BlockSpec((OUT_TM, D_MODEL), lambda i: (i, 0)),
        out_shape=jax.ShapeDtypeStruct((m, D_MODEL), jnp.float32),
        compiler_params=pltpu.CompilerParams(
            dimension_semantics=("arbitrary",), vmem_limit_bytes=VMEM_LIMIT),
        name="out_projection",
    )(ya, yb, w_out_bf, x2, mod, ln_g, ln_b)


def kernel(x, c, w_ada, b_ada, w_in, sgu_norm_g, sgu_norm_b, w_spatial, b_spatial, rpb, w_out, ln_g, ln_b):
    batch, seq_len, _ = x.shape
    assert seq_len % PROJ_TM == 0 and seq_len % OUT_TM == 0 and seq_len % (GRID_W * NA_KH) == 0
    x2 = x.reshape(batch * seq_len, D_MODEL)
    c_pad = jnp.pad(c, ((0, 8 - batch), (0, 0)))
    mod = _ada_modulation(c_pad, w_ada, b_ada).reshape(DEPTH, c_pad.shape[0], 1, 3 * D_MODEL)
    tables = _bias_tables(rpb)
    w_in_bf = w_in.astype(jnp.bfloat16)
    w_out_bf = w_out.astype(jnp.bfloat16)
    ws_bf = w_spatial.astype(jnp.bfloat16)
    bs_full = jnp.broadcast_to(b_spatial[..., None], (DEPTH, SGU_GROUPS, SGU_CHUNK, SGU_GROUP))
    norm_g = sgu_norm_g.reshape(DEPTH, 1, D_SGU)
    norm_b = sgu_norm_b.reshape(DEPTH, 1, D_SGU)
    ln_g3 = ln_g.reshape(DEPTH, 1, D_MODEL)
    ln_b3 = ln_b.reshape(DEPTH, 1, D_MODEL)
    for l in range(DEPTH):
        ya, na = _in_projection(l, x2, mod, w_in_bf, norm_g, norm_b, ws_bf, bs_full, seq_len)
        yb = _attention(l, na, tables, batch, seq_len)
        x2 = _out_projection(l, ya, yb, w_out_bf, x2, mod, ln_g3, ln_b3, seq_len)
    return x2.reshape(batch, seq_len, D_MODEL)
```

```python
import functools

import numpy as np
import jax
import jax.numpy as jnp
from jax import lax
from jax.experimental import pallas as pl
from jax.experimental.pallas import tpu as pltpu

D_MODEL = 2048
DEPTH = 2
D_SGU = 1024
D_NA = 1024
SGU_CHUNK = 128
SGU_GROUP = 128
SGU_GROUPS = D_SGU // SGU_GROUP
NA_HEAD_DIM = 64
NA_HEADS = D_NA // NA_HEAD_DIM
GRID_W = 64
NA_KH = 8
NA_KW = 16
D_IN = 3 * D_SGU + 4 * D_NA
DEEPNORM_ALPHA = (2 * DEPTH) ** 0.25
LN_EPS = 1e-5
NEG_INF = -1e30

LANES = 128
MXU_N = 256

ADA_TN = 1024
PROJ_TM = 256
PROJ_TN = MXU_N
SGU_STEPS = D_SGU // PROJ_TN
ROLE_Q, ROLE_K, ROLE_V, ROLE_GATE = range(4)
LN_ROWS = 16
LN_UNROLL = 4
OUT_TM = 256
OUT_LN_ROWS = 32
HEADS_PER_STEP = LANES // NA_HEAD_DIM
RHO_TILES = 2 * NA_KH - 2
ATTN_G = 4
VMEM_LIMIT = 56 * 1024 * 1024


def _silu(v):
    return v * jax.nn.sigmoid(v)


def _gelu(v):
    return 0.5 * v * (1.0 + lax.erf(v * (2.0 ** -0.5)))


def _ada_kernel(c_ref, w_ref, b_ref, o_ref):
    s = _silu(c_ref[...]).astype(jnp.bfloat16)
    w = w_ref[0].astype(jnp.bfloat16)
    o_ref[0] = jnp.dot(s, w, preferred_element_type=jnp.float32) + b_ref[0]


def _ada_modulation(c_pad, w_ada, b_ada):
    rows = c_pad.shape[0]
    n_out = w_ada.shape[-1]
    return pl.pallas_call(
        _ada_kernel,
        grid=(DEPTH, n_out // ADA_TN),
        in_specs=[
            pl.BlockSpec((rows, D_MODEL), lambda l, j: (0, 0)),
            pl.BlockSpec((1, D_MODEL, ADA_TN), lambda l, j: (l, 0, j)),
            pl.BlockSpec((1, 1, ADA_TN), lambda l, j: (l, 0, j)),
        ],
        out_specs=pl.BlockSpec((1, rows, ADA_TN), lambda l, j: (l, 0, j)),
        out_shape=jax.ShapeDtypeStruct((DEPTH, rows, n_out), jnp.float32),
        compiler_params=pltpu.CompilerParams(
            dimension_semantics=("arbitrary", "arbitrary"), vmem_limit_bytes=VMEM_LIMIT),
        name="ada_modulation",
    )(c_pad, w_ada, b_ada.reshape(DEPTH, 1, n_out))


def _proj_kernel(xf_ref, xn_ref, modf_ref, modn_ref, w_ref, ng_ref, nb_ref, ws_ref, bs_ref,
                 ya_ref, na_ref, h_ref):
    i = pl.program_id(0)

    def layer_norm_rows(src_ref, rows, dst_ref, m_ref):
        xv = src_ref[rows, :]
        mu = jnp.mean(xv, axis=-1, keepdims=True)
        d = xv - mu
        var = jnp.mean(d * d, axis=-1, keepdims=True)
        hn = d * lax.rsqrt(var + LN_EPS)
        dst_ref[rows, :] = (hn * (1.0 + m_ref[:, D_MODEL:2 * D_MODEL]) + m_ref[:, 0:D_MODEL]).astype(jnp.bfloat16)

    @pl.when(i == 0)
    def _first_block():
        def ln_chunk(ci, carry):
            rows = pl.ds(pl.multiple_of(ci * LN_ROWS, LN_ROWS), LN_ROWS)
            layer_norm_rows(xf_ref, rows, h_ref.at[0], modf_ref)
            return carry

        lax.fori_loop(0, PROJ_TM // LN_ROWS, ln_chunk, 0, unroll=LN_UNROLL)

    def sgu_dot(h, step):
        w = jnp.concatenate(
            [w_ref[:, part * D_SGU + step * PROJ_TN:part * D_SGU + (step + 1) * PROJ_TN] for part in range(3)],
            axis=1)
        return jnp.dot(h, w, preferred_element_type=jnp.float32)

    def sgu_epilogue(uvz, step):
        for ci in range(PROJ_TM // SGU_CHUNK):
            rows = slice(ci * SGU_CHUNK, (ci + 1) * SGU_CHUNK)
            for g in range(PROJ_TN // SGU_GROUP):
                grp = step * (PROJ_TN // SGU_GROUP) + g
                gcols = slice(grp * SGU_GROUP, (grp + 1) * SGU_GROUP)
                u, v, z = (uvz[rows, part * PROJ_TN + g * SGU_GROUP:part * PROJ_TN + (g + 1) * SGU_GROUP]
                           for part in range(3))
                gv = _gelu(v)
                mu = jnp.mean(gv, axis=-1, keepdims=True)
                d = gv - mu
                var = jnp.mean(d * d, axis=-1, keepdims=True)
                vn = d * lax.rsqrt(var + LN_EPS) * ng_ref[:, gcols] + nb_ref[:, gcols]
                sv = jnp.dot(ws_ref[grp], vn.astype(jnp.bfloat16),
                             preferred_element_type=jnp.float32) + bs_ref[grp]
                ya_ref[rows, gcols] = (_gelu(u) * sv * _silu(z)).astype(jnp.bfloat16)

    def na_dot(h, role):
        return jnp.dot(h, w_ref[:, 3 * D_SGU + role * D_NA:3 * D_SGU + (role + 1) * D_NA],
                       preferred_element_type=jnp.float32)

    def na_epilogue(r, role):
        if role == ROLE_Q:
            r = r * (NA_HEAD_DIM ** -0.5)
        elif role == ROLE_GATE:
            r = _silu(r)
        na_ref[:, role * D_NA:(role + 1) * D_NA] = r.astype(jnp.bfloat16)

    units = [("sgu", s) for s in range(SGU_STEPS)] + [("na", r) for r in (ROLE_GATE, ROLE_Q, ROLE_K, ROLE_V)]
    ln_units = len(units) - 2
    ln_chunks = PROJ_TM // LN_ROWS
    for parity in range(2):
        @pl.when(i % 2 == parity)
        def _block(parity=parity):
            h = h_ref[parity]

            def unit_dot(n):
                kind, idx = units[n]
                return sgu_dot(h, idx) if kind == "sgu" else na_dot(h, idx)

            done = 0
            acc = unit_dot(0)
            for n, (kind, idx) in enumerate(units):
                nxt_acc = unit_dot(n + 1) if n + 1 < len(units) else None
                if kind == "sgu":
                    sgu_epilogue(acc, idx)
                else:
                    na_epilogue(acc, idx)
                acc = nxt_acc
                upto = min(ln_chunks, (n + 1) * ln_chunks // ln_units)
                for ci in range(done, upto):
                    rows = slice(ci * LN_ROWS, (ci + 1) * LN_ROWS)
                    layer_norm_rows(xn_ref, rows, h_ref.at[1 - parity], modn_ref)
                done = upto


def _in_projection(layer, x2, mod, w_in_bf, norm_g, norm_b, ws_bf, bs_full, seq_len):
    m = x2.shape[0]
    n_blocks = m // PROJ_TM
    blocks_per_batch = seq_len // PROJ_TM
    nxt = lambda i: jnp.minimum(i + 1, n_blocks - 1)
    resident = pl.Buffered(1)
    return pl.pallas_call(
        _proj_kernel,
        grid=(n_blocks,),
        in_specs=[
            pl.BlockSpec((PROJ_TM, D_MODEL), lambda i: (0, 0), pipeline_mode=resident),
            pl.BlockSpec((PROJ_TM, D_MODEL), lambda i: (nxt(i), 0)),
            pl.BlockSpec((None, None, 1, 3 * D_MODEL), lambda i: (layer, 0, 0, 0)),
            pl.BlockSpec((None, None, 1, 3 * D_MODEL), lambda i: (layer, nxt(i) // blocks_per_batch, 0, 0)),
            pl.BlockSpec((None, D_MODEL, D_IN), lambda i: (layer, 0, 0), pipeline_mode=resident),
            pl.BlockSpec((None, 1, D_SGU), lambda i: (layer, 0, 0)),
            pl.BlockSpec((None, 1, D_SGU), lambda i: (layer, 0, 0)),
            pl.BlockSpec((None, SGU_GROUPS, SGU_CHUNK, SGU_CHUNK), lambda i: (layer, 0, 0, 0)),
            pl.BlockSpec((None, SGU_GROUPS, SGU_CHUNK, SGU_GROUP), lambda i: (layer, 0, 0, 0)),
        ],
        out_specs=[
            pl.BlockSpec((PROJ_TM, D_SGU), lambda i: (i, 0)),
            pl.BlockSpec((PROJ_TM, 4 * D_NA), lambda i: (i, 0)),
        ],
        out_shape=[
            jax.ShapeDtypeStruct((m, D_SGU), jnp.bfloat16),
            jax.ShapeDtypeStruct((m, 4 * D_NA), jnp.bfloat16),
        ],
        scratch_shapes=[pltpu.VMEM((2, PROJ_TM, D_MODEL), jnp.bfloat16)],
        compiler_params=pltpu.CompilerParams(
            dimension_semantics=("arbitrary",), vmem_limit_bytes=VMEM_LIMIT),
        name="in_projection",
    )(x2, x2, mod, mod, w_in_bf, norm_g, norm_b, ws_bf, bs_full)


def _attn_kernel(q_ref, k_ref, v_ref, z_ref, r_ref, y_ref, s_ref, p_ref, t_ref, *, rows):
    lane = lax.broadcasted_iota(jnp.int32, (GRID_W, LANES), 1)
    first_head = lane < NA_HEAD_DIM
    win = NA_KH * GRID_W
    ones = jnp.ones((win, LANES), jnp.bfloat16)
    n_groups = rows // ATTN_G

    qi = lax.broadcasted_iota(jnp.int32, (GRID_W, LANES), 0)
    xi = lane & (GRID_W - 1)
    win_start = jnp.clip(qi - NA_KW // 2, 0, GRID_W - NA_KW)
    in_window = (xi >= win_start) & (xi < win_start + NA_KW)
    for hh in range(HEADS_PER_STEP):
        for rho in range(RHO_TILES):
            row = jnp.broadcast_to(r_ref[hh, rho:rho + 1, :], (GRID_W, LANES))
            diag = pltpu.roll(row, LANES - (NA_KW - 1), 1, stride=1, stride_axis=0)
            t_ref[hh, rho] = jnp.where(in_window, diag, NEG_INF)

    def window(r):
        rs = jnp.clip(r - NA_KH // 2, 0, rows - NA_KH)
        return rs, pl.ds(pl.multiple_of(rs * GRID_W, GRID_W), win)

    def scores(group, slot):
        for g in range(ATTN_G):
            r = group * ATTN_G + g
            rs, k_rows = window(r)
            rho0 = rs - r + NA_KH - 1
            q = q_ref[pl.ds(pl.multiple_of(r * GRID_W, GRID_W), GRID_W), :]
            zero = jnp.zeros_like(q)
            qq = jnp.concatenate([jnp.where(first_head, q, zero), jnp.where(first_head, zero, q)], axis=0)
            s = lax.dot_general(qq, k_ref[k_rows, :], (((1,), (1,)), ((), ())),
                                preferred_element_type=jnp.float32)
            bias = jnp.concatenate(
                [jnp.concatenate([t_ref[hh, rho0 + 2 * jt] for jt in range(NA_KH // 2)], axis=1)
                 for hh in range(HEADS_PER_STEP)], axis=0)
            s_ref[slot, g] = s + bias

    def probs(slot):
        for g in range(ATTN_G):
            s = s_ref[slot, g]
            mx = jnp.max(s, axis=-1, keepdims=True)
            p_ref[slot, g] = jnp.exp(s - mx).astype(jnp.bfloat16)

    def apply(group, slot):
        for g in range(ATTN_G):
            r = group * ATTN_G + g
            _, k_rows = window(r)
            q_rows = pl.ds(pl.multiple_of(r * GRID_W, GRID_W), GRID_W)
            v_aug = jnp.concatenate([v_ref[k_rows, :], ones], axis=1)
            oa = jnp.dot(p_ref[slot, g], v_aug, preferred_element_type=jnp.float32)
            o = oa[:, :LANES] / oa[:, LANES:]
            out = jnp.where(first_head, o[:GRID_W], o[GRID_W:])
            y_ref[q_rows, :] = (out * z_ref[q_rows, :].astype(jnp.float32)).astype(jnp.bfloat16)

    def phase(t, cur, do_scores=True, do_apply=True):
        nxt = 1 - cur
        if do_scores:
            scores(t + 1, nxt)
        probs(cur)
        if do_apply:
            apply(t - 1, nxt)

    scores(0, 0)
    phase(0, 0, do_apply=False)

    def body(tt, carry):
        t = 2 * tt + 1
        phase(t, 1)
        phase(t + 1, 0)
        return carry

    lax.fori_loop(0, (n_groups - 2) // 2, body, 0)
    phase(n_groups - 1, 1, do_scores=False)
    apply(n_groups - 1, 1)


def _attention(layer, na, rpb_rows, batch, seq_len):
    rows = seq_len // GRID_W
    n_hp = NA_HEADS // HEADS_PER_STEP
    assert (rows // ATTN_G) % 2 == 0 and rows % ATTN_G == 0
    col = lambda part: (lambda b, hp: (b, part * n_hp + hp))
    return pl.pallas_call(
        functools.partial(_attn_kernel, rows=rows),
        scratch_shapes=[
            pltpu.VMEM((2, ATTN_G, HEADS_PER_STEP * GRID_W, NA_KH * GRID_W), jnp.float32),
            pltpu.VMEM((2, ATTN_G, HEADS_PER_STEP * GRID_W, NA_KH * GRID_W), jnp.bfloat16),
            pltpu.VMEM((HEADS_PER_STEP, RHO_TILES, GRID_W, LANES), jnp.float32),
        ],
        grid=(batch, n_hp),
        in_specs=[
            pl.BlockSpec((seq_len, LANES), col(0)),
            pl.BlockSpec((seq_len, LANES), col(1)),
            pl.BlockSpec((seq_len, LANES), col(2)),
            pl.BlockSpec((seq_len, LANES), col(3)),
            pl.BlockSpec((None, HEADS_PER_STEP, RHO_TILES, LANES), lambda b, hp: (layer, hp, 0, 0)),
        ],
        out_specs=pl.BlockSpec((seq_len, LANES), lambda b, hp: (b, hp)),
        out_shape=jax.ShapeDtypeStruct((batch * seq_len, D_NA), jnp.bfloat16),
        compiler_params=pltpu.CompilerParams(
            dimension_semantics=("arbitrary", "arbitrary"), vmem_limit_bytes=VMEM_LIMIT),
        name="neighbourhood_attention",
    )(na, na, na, na, rpb_rows)


def _rpb_row_pairs(rpb):
    padded = jnp.pad(rpb, ((0, 0), (0, 0), (0, 0), (0, GRID_W - rpb.shape[-1])))
    return jnp.concatenate([padded[:, :, :-1], padded[:, :, 1:]], axis=-1)


def _out_kernel(ya_ref, yb_ref, w_ref, x_ref, mod_ref, g_ref, b_ref, o_ref, y_ref):
    i = pl.program_id(0)

    @pl.when(i == 0)
    def _init():
        y_ref[1] = jnp.zeros((OUT_TM, D_MODEL), jnp.float32)

    for parity in range(2):
        @pl.when(i % 2 == parity)
        def _step(parity=parity):
            y = jnp.dot(ya_ref[...], w_ref[0:D_SGU, :], preferred_element_type=jnp.float32)
            y_ref[parity] = y + jnp.dot(yb_ref[...], w_ref[D_SGU:, :], preferred_element_type=jnp.float32)
            gate1 = 1.0 + mod_ref[:, 2 * D_MODEL:3 * D_MODEL]
            for piece in range(OUT_TM // OUT_LN_ROWS):
                rows = slice(piece * OUT_LN_ROWS, (piece + 1) * OUT_LN_ROWS)
                t = DEEPNORM_ALPHA * x_ref[rows, :] + gate1 * y_ref[1 - parity, rows, :]
                mu = jnp.mean(t, axis=-1, keepdims=True)
                d = t - mu
                var = jnp.mean(d * d, axis=-1, keepdims=True)
                o_ref[rows, :] = d * lax.rsqrt(var + LN_EPS) * g_ref[...] + b_ref[...]


def _out_projection(layer, ya, yb, w_out_bf, x2, mod, ln_g, ln_b, seq_len):
    m = x2.shape[0]
    n_blocks = m // OUT_TM
    blocks_per_batch = seq_len // OUT_TM
    cur = lambda i: jnp.minimum(i, n_blocks - 1)
    prev = lambda i: jnp.maximum(i - 1, 0)
    return pl.pallas_call(
        _out_kernel,
        grid=(n_blocks + 1,),
        in_specs=[
            pl.BlockSpec((OUT_TM, D_SGU), lambda i: (cur(i), 0)),
            pl.BlockSpec((OUT_TM, D_NA), lambda i: (cur(i), 0)),
            pl.BlockSpec((None, D_SGU + D_NA, D_MODEL), lambda i: (layer, 0, 0), pipeline_mode=pl.Buffered(1)),
            pl.BlockSpec((OUT_TM, D_MODEL), lambda i: (prev(i), 0)),
            pl.BlockSpec((None, None, 1, 3 * D_MODEL), lambda i: (layer, prev(i) // blocks_per_batch, 0, 0)),
            pl.BlockSpec((None, 1, D_MODEL), lambda i: (layer, 0, 0)),
            pl.BlockSpec((None, 1, D_MODEL), lambda i: (layer, 0, 0)),
        ],
        out_specs=pl.BlockSpec((OUT_TM, D_MODEL), lambda i: (prev(i), 0)),
        out_shape=jax.ShapeDtypeStruct((m, D_MODEL), jnp.float32),
        scratch_shapes=[pltpu.VMEM((2, OUT_TM, D_MODEL), jnp.float32)],
        compiler_params=pltpu.CompilerParams(
            dimension_semantics=("arbitrary",), vmem_limit_bytes=VMEM_LIMIT),
        name="out_projection",
    )(ya, yb, w_out_bf, x2, mod, ln_g, ln_b)


def kernel(x, c, w_ada, b_ada, w_in, sgu_norm_g, sgu_norm_b, w_spatial, b_spatial, rpb, w_out, ln_g, ln_b):
    batch, seq_len, _ = x.shape
    assert seq_len % PROJ_TM == 0 and seq_len % OUT_TM == 0 and seq_len % (GRID_W * NA_KH) == 0
    x2 = x.reshape(batch * seq_len, D_MODEL)
    c_pad = jnp.pad(c, ((0, 8 - batch), (0, 0)))
    mod = _ada_modulation(c_pad, w_ada, b_ada).reshape(DEPTH, c_pad.shape[0], 1, 3 * D_MODEL)
    rpb_rows = _rpb_row_pairs(rpb)
    w_in_bf = w_in.astype(jnp.bfloat16)
    w_out_bf = w_out.astype(jnp.bfloat16)
    ws_bf = w_spatial.astype(jnp.bfloat16)
    bs_full = jnp.broadcast_to(b_spatial[..., None], (DEPTH, SGU_GROUPS, SGU_CHUNK, SGU_GROUP))
    norm_g = sgu_norm_g.reshape(DEPTH, 1, D_SGU)
    norm_b = sgu_norm_b.reshape(DEPTH, 1, D_SGU)
    ln_g3 = ln_g.reshape(DEPTH, 1, D_MODEL)
    ln_b3 = ln_b.reshape(DEPTH, 1, D_MODEL)
    for l in range(DEPTH):
        ya, na = _in_projection(l, x2, mod, w_in_bf, norm_g, norm_b, ws_bf, bs_full, seq_len)
        yb = _attention(l, na, rpb_rows, batch, seq_len)
        x2 = _out_projection(l, ya, yb, w_out_bf, x2, mod, ln_g3, ln_b3, seq_len)
    return x2.reshape(batch, seq_len, D_MODEL)
```

```python
import functools

import jax
import jax.numpy as jnp
from jax import lax
from jax.experimental import pallas as pl
from jax.experimental.pallas import tpu as pltpu

D_MODEL = 2048
DEPTH = 2
D_SGU = 1024
D_NA = 1024
SGU_CHUNK = 128
SGU_GROUP = 128
SGU_GROUPS = D_SGU // SGU_GROUP
NA_HEAD_DIM = 64
NA_HEADS = D_NA // NA_HEAD_DIM
GRID_W = 64
NA_KH = 8
NA_KW = 16
D_IN = 3 * D_SGU + 4 * D_NA
DEEPNORM_ALPHA = (2 * DEPTH) ** 0.25
LN_EPS = 1e-5
NEG_INF = -1e30

LANES = 128
MXU_N = 256

ADA_TN = 1024
PROJ_TM = 256
PROJ_TN = MXU_N
SGU_STEPS = D_SGU // PROJ_TN
ROLE_Q, ROLE_K, ROLE_V, ROLE_GATE = range(4)
LN_ROWS = 16
LN_UNROLL = 4
OUT_TM = 512
OUT_TN = 1024
OUT_LN_ROWS = 32
HEADS_PER_STEP = LANES // NA_HEAD_DIM
RHO_TILES = 2 * NA_KH - 2
ATTN_G = 4
VMEM_LIMIT = 56 * 1024 * 1024


def _silu(v):
    return v * jax.nn.sigmoid(v)


def _gelu(v):
    return 0.5 * v * (1.0 + lax.erf(v * (2.0 ** -0.5)))


def _ada_kernel(c_ref, w_ref, b_ref, o_ref):
    s = _silu(c_ref[...]).astype(jnp.bfloat16)
    w = w_ref[0].astype(jnp.bfloat16)
    o_ref[0] = jnp.dot(s, w, preferred_element_type=jnp.float32) + b_ref[0]


def _ada_modulation(c_pad, w_ada, b_ada):
    rows = c_pad.shape[0]
    n_out = w_ada.shape[-1]
    return pl.pallas_call(
        _ada_kernel,
        grid=(DEPTH, n_out // ADA_TN),
        in_specs=[
            pl.BlockSpec((rows, D_MODEL), lambda l, j: (0, 0)),
            pl.BlockSpec((1, D_MODEL, ADA_TN), lambda l, j: (l, 0, j)),
            pl.BlockSpec((1, 1, ADA_TN), lambda l, j: (l, 0, j)),
        ],
        out_specs=pl.BlockSpec((1, rows, ADA_TN), lambda l, j: (l, 0, j)),
        out_shape=jax.ShapeDtypeStruct((DEPTH, rows, n_out), jnp.float32),
        compiler_params=pltpu.CompilerParams(
            dimension_semantics=("arbitrary", "arbitrary"), vmem_limit_bytes=VMEM_LIMIT),
        name="ada_modulation",
    )(c_pad, w_ada, b_ada.reshape(DEPTH, 1, n_out))


def _proj_kernel(xf_ref, xn_ref, modf_ref, modn_ref, w_ref, ng_ref, nb_ref, ws_ref, bs_ref,
                 ya_ref, na_ref, h_ref):
    i = pl.program_id(0)

    def layer_norm_rows(src_ref, rows, dst_ref, m_ref):
        xv = src_ref[rows, :]
        mu = jnp.mean(xv, axis=-1, keepdims=True)
        d = xv - mu
        var = jnp.mean(d * d, axis=-1, keepdims=True)
        hn = d * lax.rsqrt(var + LN_EPS)
        dst_ref[rows, :] = (hn * (1.0 + m_ref[:, D_MODEL:2 * D_MODEL]) + m_ref[:, 0:D_MODEL]).astype(jnp.bfloat16)

    @pl.when(i == 0)
    def _first_block():
        def ln_chunk(ci, carry):
            rows = pl.ds(pl.multiple_of(ci * LN_ROWS, LN_ROWS), LN_ROWS)
            layer_norm_rows(xf_ref, rows, h_ref.at[0], modf_ref)
            return carry

        lax.fori_loop(0, PROJ_TM // LN_ROWS, ln_chunk, 0, unroll=LN_UNROLL)

    def sgu_dot(h, step):
        w = jnp.concatenate(
            [w_ref[:, part * D_SGU + step * PROJ_TN:part * D_SGU + (step + 1) * PROJ_TN] for part in range(3)],
            axis=1)
        return jnp.dot(h, w, preferred_element_type=jnp.float32)

    def sgu_epilogue(uvz, step):
        for ci in range(PROJ_TM // SGU_CHUNK):
            rows = slice(ci * SGU_CHUNK, (ci + 1) * SGU_CHUNK)
            for g in range(PROJ_TN // SGU_GROUP):
                grp = step * (PROJ_TN // SGU_GROUP) + g
                gcols = slice(grp * SGU_GROUP, (grp + 1) * SGU_GROUP)
                u, v, z = (uvz[rows, part * PROJ_TN + g * SGU_GROUP:part * PROJ_TN + (g + 1) * SGU_GROUP]
                           for part in range(3))
                gv = _gelu(v)
                mu = jnp.mean(gv, axis=-1, keepdims=True)
                d = gv - mu
                var = jnp.mean(d * d, axis=-1, keepdims=True)
                vn = d * lax.rsqrt(var + LN_EPS) * ng_ref[:, gcols] + nb_ref[:, gcols]
                sv = jnp.dot(ws_ref[grp], vn.astype(jnp.bfloat16),
                             preferred_element_type=jnp.float32) + bs_ref[grp]
                ya_ref[rows, gcols] = (_gelu(u) * sv * _silu(z)).astype(jnp.bfloat16)

    def na_dot(h, role):
        return jnp.dot(h, w_ref[:, 3 * D_SGU + role * D_NA:3 * D_SGU + (role + 1) * D_NA],
                       preferred_element_type=jnp.float32)

    def na_epilogue(r, role):
        if role == ROLE_Q:
            r = r * (NA_HEAD_DIM ** -0.5)
        elif role == ROLE_GATE:
            r = _silu(r)
        na_ref[:, role * D_NA:(role + 1) * D_NA] = r.astype(jnp.bfloat16)

    units = [("sgu", s) for s in range(SGU_STEPS)] + [("na", r) for r in (ROLE_GATE, ROLE_Q, ROLE_K, ROLE_V)]
    ln_units = len(units) - 2
    ln_chunks = PROJ_TM // LN_ROWS
    for parity in range(2):
        @pl.when(i % 2 == parity)
        def _block(parity=parity):
            h = h_ref[parity]

            def unit_dot(n):
                kind, idx = units[n]
                return sgu_dot(h, idx) if kind == "sgu" else na_dot(h, idx)

            done = 0
            acc = unit_dot(0)
            for n, (kind, idx) in enumerate(units):
                nxt_acc = unit_dot(n + 1) if n + 1 < len(units) else None
                if kind == "sgu":
                    sgu_epilogue(acc, idx)
                else:
                    na_epilogue(acc, idx)
                acc = nxt_acc
                upto = min(ln_chunks, (n + 1) * ln_chunks // ln_units)
                for ci in range(done, upto):
                    rows = slice(ci * LN_ROWS, (ci + 1) * LN_ROWS)
                    layer_norm_rows(xn_ref, rows, h_ref.at[1 - parity], modn_ref)
                done = upto


def _in_projection(layer, x2, mod, w_in_bf, norm_g, norm_b, ws_bf, bs_full, seq_len):
    m = x2.shape[0]
    n_blocks = m // PROJ_TM
    blocks_per_batch = seq_len // PROJ_TM
    nxt = lambda i: jnp.minimum(i + 1, n_blocks - 1)
    resident = pl.Buffered(1)
    return pl.pallas_call(
        _proj_kernel,
        grid=(n_blocks,),
        in_specs=[
            pl.BlockSpec((PROJ_TM, D_MODEL), lambda i: (0, 0), pipeline_mode=resident),
            pl.BlockSpec((PROJ_TM, D_MODEL), lambda i: (nxt(i), 0)),
            pl.BlockSpec((None, None, 1, 3 * D_MODEL), lambda i: (layer, 0, 0, 0)),
            pl.BlockSpec((None, None, 1, 3 * D_MODEL), lambda i: (layer, nxt(i) // blocks_per_batch, 0, 0)),
            pl.BlockSpec((None, D_MODEL, D_IN), lambda i: (layer, 0, 0), pipeline_mode=resident),
            pl.BlockSpec((None, 1, D_SGU), lambda i: (layer, 0, 0)),
            pl.BlockSpec((None, 1, D_SGU), lambda i: (layer, 0, 0)),
            pl.BlockSpec((None, SGU_GROUPS, SGU_CHUNK, SGU_CHUNK), lambda i: (layer, 0, 0, 0)),
            pl.BlockSpec((None, SGU_GROUPS, SGU_CHUNK, SGU_GROUP), lambda i: (layer, 0, 0, 0)),
        ],
        out_specs=[
            pl.BlockSpec((PROJ_TM, D_SGU), lambda i: (i, 0)),
            pl.BlockSpec((PROJ_TM, 4 * D_NA), lambda i: (i, 0)),
        ],
        out_shape=[
            jax.ShapeDtypeStruct((m, D_SGU), jnp.bfloat16),
            jax.ShapeDtypeStruct((m, 4 * D_NA), jnp.bfloat16),
        ],
        scratch_shapes=[pltpu.VMEM((2, PROJ_TM, D_MODEL), jnp.bfloat16)],
        compiler_params=pltpu.CompilerParams(
            dimension_semantics=("arbitrary",), vmem_limit_bytes=VMEM_LIMIT),
        name="in_projection",
    )(x2, x2, mod, mod, w_in_bf, norm_g, norm_b, ws_bf, bs_full)


def _attn_kernel(q_ref, k_ref, v_ref, z_ref, r_ref, y_ref, s_ref, p_ref, t_ref, *, rows):
    lane = lax.broadcasted_iota(jnp.int32, (GRID_W, LANES), 1)
    first_head = lane < NA_HEAD_DIM
    win = NA_KH * GRID_W
    ones = jnp.ones((win, LANES), jnp.bfloat16)
    n_groups = rows // ATTN_G

    qi = lax.broadcasted_iota(jnp.int32, (GRID_W, LANES), 0)
    xi = lane & (GRID_W - 1)
    win_start = jnp.clip(qi - NA_KW // 2, 0, GRID_W - NA_KW)
    in_window = (xi >= win_start) & (xi < win_start + NA_KW)
    for hh in range(HEADS_PER_STEP):
        for rho in range(RHO_TILES):
            row = jnp.broadcast_to(r_ref[hh, rho:rho + 1, :], (GRID_W, LANES))
            diag = pltpu.roll(row, LANES - (NA_KW - 1), 1, stride=1, stride_axis=0)
            t_ref[hh, rho] = jnp.where(in_window, diag, NEG_INF)

    def window(r):
        rs = jnp.clip(r - NA_KH // 2, 0, rows - NA_KH)
        return rs, pl.ds(pl.multiple_of(rs * GRID_W, GRID_W), win)

    def scores(group, slot):
        for g in range(ATTN_G):
            r = group * ATTN_G + g
            rs, k_rows = window(r)
            rho0 = rs - r + NA_KH - 1
            q = q_ref[pl.ds(pl.multiple_of(r * GRID_W, GRID_W), GRID_W), :]
            zero = jnp.zeros_like(q)
            qq = jnp.concatenate([jnp.where(first_head, q, zero), jnp.where(first_head, zero, q)], axis=0)
            s = lax.dot_general(qq, k_ref[k_rows, :], (((1,), (1,)), ((), ())),
                                preferred_element_type=jnp.float32)
            bias = jnp.concatenate(
                [jnp.concatenate([t_ref[hh, rho0 + 2 * jt] for jt in range(NA_KH // 2)], axis=1)
                 for hh in range(HEADS_PER_STEP)], axis=0)
            s_ref[slot, g] = s + bias

    def probs(slot):
        for g in range(ATTN_G):
            s = s_ref[slot, g]
            mx = jnp.max(s, axis=-1, keepdims=True)
            p_ref[slot, g] = jnp.exp(s - mx).astype(jnp.bfloat16)

    def apply(group, slot):
        for g in range(ATTN_G):
            r = group * ATTN_G + g
            _, k_rows = window(r)
            q_rows = pl.ds(pl.multiple_of(r * GRID_W, GRID_W), GRID_W)
            v_aug = jnp.concatenate([v_ref[k_rows, :], ones], axis=1)
            oa = jnp.dot(p_ref[slot, g], v_aug, preferred_element_type=jnp.float32)
            o = oa[:, :LANES] / oa[:, LANES:]
            out = jnp.where(first_head, o[:GRID_W], o[GRID_W:])
            y_ref[q_rows, :] = (out * z_ref[q_rows, :].astype(jnp.float32)).astype(jnp.bfloat16)

    def phase(t, cur, do_scores=True, do_apply=True):
        nxt = 1 - cur
        if do_scores:
            scores(t + 1, nxt)
        probs(cur)
        if do_apply:
            apply(t - 1, nxt)

    scores(0, 0)
    phase(0, 0, do_apply=False)

    def body(tt, carry):
        t = 2 * tt + 1
        phase(t, 1)
        phase(t + 1, 0)
        return carry

    lax.fori_loop(0, (n_groups - 2) // 2, body, 0)
    phase(n_groups - 1, 1, do_scores=False)
    apply(n_groups - 1, 1)


def _attention(layer, na, rpb_rows, batch, seq_len):
    rows = seq_len // GRID_W
    n_hp = NA_HEADS // HEADS_PER_STEP
    assert (rows // ATTN_G) % 2 == 0 and rows % ATTN_G == 0
    col = lambda part: (lambda b, hp: (b, part * n_hp + hp))
    return pl.pallas_call(
        functools.partial(_attn_kernel, rows=rows),
        scratch_shapes=[
            pltpu.VMEM((2, ATTN_G, HEADS_PER_STEP * GRID_W, NA_KH * GRID_W), jnp.float32),
            pltpu.VMEM((2, ATTN_G, HEADS_PER_STEP * GRID_W, NA_KH * GRID_W), jnp.bfloat16),
            pltpu.VMEM((HEADS_PER_STEP, RHO_TILES, GRID_W, LANES), jnp.float32),
        ],
        grid=(batch, n_hp),
        in_specs=[
            pl.BlockSpec((seq_len, LANES), col(0)),
            pl.BlockSpec((seq_len, LANES), col(1)),
            pl.BlockSpec((seq_len, LANES), col(2)),
            pl.BlockSpec((seq_len, LANES), col(3)),
            pl.BlockSpec((None, HEADS_PER_STEP, RHO_TILES, LANES), lambda b, hp: (layer, hp, 0, 0)),
        ],
        out_specs=pl.BlockSpec((seq_len, LANES), lambda b, hp: (b, hp)),
        out_shape=jax.ShapeDtypeStruct((batch * seq_len, D_NA), jnp.bfloat16),
        compiler_params=pltpu.CompilerParams(
            dimension_semantics=("arbitrary", "arbitrary"), vmem_limit_bytes=VMEM_LIMIT),
        name="neighbourhood_attention",
    )(na, na, na, na, rpb_rows)


def _rpb_row_pairs(rpb):
    padded = jnp.pad(rpb, ((0, 0), (0, 0), (0, 0), (0, GRID_W - rpb.shape[-1])))
    return jnp.concatenate([padded[:, :, :-1], padded[:, :, 1:]], axis=-1)


def _out_kernel(ya_ref, yb_ref, w_ref, x_ref, mod_ref, g_ref, b_ref, o_ref, y_ref):
    i = pl.program_id(0)

    @pl.when(i == 0)
    def _init():
        y_ref[1] = jnp.zeros((OUT_TM, D_MODEL), jnp.float32)

    for parity in range(2):
        @pl.when(i % 2 == parity)
        def _step(parity=parity):
            y_cat = jnp.concatenate([ya_ref[...], yb_ref[...]], axis=1)
            for nh in range(D_MODEL // OUT_TN):
                cols = slice(nh * OUT_TN, (nh + 1) * OUT_TN)
                y_ref[parity, :, cols] = jnp.dot(y_cat, w_ref[:, cols], preferred_element_type=jnp.float32)
            gate1 = 1.0 + mod_ref[:, 2 * D_MODEL:3 * D_MODEL]
            for piece in range(OUT_TM // OUT_LN_ROWS):
                rows = slice(piece * OUT_LN_ROWS, (piece + 1) * OUT_LN_ROWS)
                t = DEEPNORM_ALPHA * x_ref[rows, :] + gate1 * y_ref[1 - parity, rows, :]
                mu = jnp.mean(t, axis=-1, keepdims=True)
                d = t - mu
                var = jnp.mean(d * d, axis=-1, keepdims=True)
                o_ref[rows, :] = d * lax.rsqrt(var + LN_EPS) * g_ref[...] + b_ref[...]


def _out_projection(layer, ya, yb, w_out_bf, x2, mod, ln_g, ln_b, seq_len):
    m = x2.shape[0]
    n_blocks = m // OUT_TM
    blocks_per_batch = seq_len // OUT_TM
    cur = lambda i: jnp.minimum(i, n_blocks - 1)
    prev = lambda i: jnp.maximum(i - 1, 0)
    return pl.pallas_call(
        _out_kernel,
        grid=(n_blocks + 1,),
        in_specs=[
            pl.BlockSpec((OUT_TM, D_SGU), lambda i: (cur(i), 0)),
            pl.BlockSpec((OUT_TM, D_NA), lambda i: (cur(i), 0)),
            pl.BlockSpec((None, D_SGU + D_NA, D_MODEL), lambda i: (layer, 0, 0), pipeline_mode=pl.Buffered(1)),
            pl.BlockSpec((OUT_TM, D_MODEL), lambda i: (prev(i), 0)),
            pl.BlockSpec((None, None, 1, 3 * D_MODEL), lambda i: (layer, prev(i) // blocks_per_batch, 0, 0)),
            pl.BlockSpec((None, 1, D_MODEL), lambda i: (layer, 0, 0)),
            pl.BlockSpec((None, 1, D_MODEL), lambda i: (layer, 0, 0)),
        ],
        out_specs=pl.BlockSpec((OUT_TM, D_MODEL), lambda i: (prev(i), 0)),
        out_shape=jax.ShapeDtypeStruct((m, D_MODEL), jnp.float32),
        scratch_shapes=[pltpu.VMEM((2, OUT_TM, D_MODEL), jnp.float32)],
        compiler_params=pltpu.CompilerParams(
            dimension_semantics=("arbitrary",), vmem_limit_bytes=VMEM_LIMIT),
        name="out_projection",
    )(ya, yb, w_out_bf, x2, mod, ln_g, ln_b)


def kernel(x, c, w_ada, b_ada, w_in, sgu_norm_g, sgu_norm_b, w_spatial, b_spatial, rpb, w_out, ln_g, ln_b):
    batch, seq_len, _ = x.shape
    assert seq_len % PROJ_TM == 0 and seq_len % OUT_TM == 0 and seq_len % (GRID_W * NA_KH) == 0
    x2 = x.reshape(batch * seq_len, D_MODEL)
    c_pad = jnp.pad(c, ((0, 8 - batch), (0, 0)))
    mod = _ada_modulation(c_pad, w_ada, b_ada).reshape(DEPTH, c_pad.shape[0], 1, 3 * D_MODEL)
    rpb_rows = _rpb_row_pairs(rpb)
    w_in_bf = w_in.astype(jnp.bfloat16)
    w_out_bf = w_out.astype(jnp.bfloat16)
    ws_bf = w_spatial.astype(jnp.bfloat16)
    bs_full = jnp.broadcast_to(b_spatial[..., None], (DEPTH, SGU_GROUPS, SGU_CHUNK, SGU_GROUP))
    norm_g = sgu_norm_g.reshape(DEPTH, 1, D_SGU)
    norm_b = sgu_norm_b.reshape(DEPTH, 1, D_SGU)
    ln_g3 = ln_g.reshape(DEPTH, 1, D_MODEL)
    ln_b3 = ln_b.reshape(DEPTH, 1, D_MODEL)
    for l in range(DEPTH):
        ya, na = _in_projection(l, x2, mod, w_in_bf, norm_g, norm_b, ws_bf, bs_full, seq_len)
        yb = _attention(l, na, rpb_rows, batch, seq_len)
        x2 = _out_projection(l, ya, yb, w_out_bf, x2, mod, ln_g3, ln_b3, seq_len)
    return x2.reshape(batch, seq_len, D_MODEL)
```
